```python
import jax, jax.numpy as jnp
from jax import lax
import numpy as np

D_MODEL = 2048
BATCH = 4
SEQ = 4096
DEPTH = 4

N_MIXERS = 4
D_FF = 4 * D_MODEL
EPS = 1e-6
NEG = -1e30
FORCE = 1e30

CONV_WIDTH = 31
NSA_HEADS = 16
NSA_HEAD_DIM = D_MODEL // NSA_HEADS
NSA_KV_GROUPS = 4
NSA_HPG = NSA_HEADS // NSA_KV_GROUPS
CMP_BLOCK = 32
CMP_STRIDE = 16
SEL_BLOCK = 64
SEL_TOPK = 16
WINDOW = 512
NSA_Q_CHUNK = 32
POOL_WINDOWS = (2, 4, 8, 16)
POOL_GROUPS = len(POOL_WINDOWS)
POOL_GROUP_DIM = D_MODEL // POOL_GROUPS
MLA_HEADS = 16
MLA_NOPE_DIM = 128
MLA_ROPE_DIM = 64
MLA_V_DIM = 128
MLA_Q_RANK = 512
MLA_KV_RANK = 256
ROPE_THETA = 10000.0
MLA_Q_CHUNK = 128

N_USES = tuple((DEPTH - m + N_MIXERS - 1) // N_MIXERS for m in range(N_MIXERS))

kernel_name = "hybrid_conv_nsa_pool_mla_trunk"


def rms_norm(x, g):
    xf = x.astype(jnp.float32)
    y = xf * lax.rsqrt(jnp.mean(xf * xf, axis=-1, keepdims=True) + EPS)
    return (y * g.astype(jnp.float32)).astype(x.dtype)


def layer_norm(x, g, b):
    xf = x.astype(jnp.float32)
    mu = jnp.mean(xf, axis=-1, keepdims=True)
    var = jnp.mean(jnp.square(xf - mu), axis=-1, keepdims=True)
    y = (xf - mu) * lax.rsqrt(var + EPS) * g.astype(jnp.float32) + b.astype(jnp.float32)
    return y.astype(x.dtype)


def masked_softmax(s, mask):
    s = jnp.where(mask, s.astype(jnp.float32), NEG)
    m = jnp.max(s, axis=-1, keepdims=True)
    e = jnp.where(mask, jnp.exp(s - m), 0.0)
    return e / jnp.maximum(jnp.sum(e, axis=-1, keepdims=True), 1e-30)


def apply_rope(x, cos, sin):
    x1, x2 = jnp.split(x.astype(jnp.float32), 2, axis=-1)
    return jnp.concatenate([x1 * cos - x2 * sin, x1 * sin + x2 * cos], axis=-1).astype(x.dtype)


def sq_relu_mlp(h, w1, w2):
    return jnp.square(jax.nn.relu(h @ w1)) @ w2


def conformer_conv(h, w_in, b_in, w_dw, b_dw, ln_g, ln_b, w_out, b_out):
    u = h @ w_in + b_in
    a, gt = jnp.split(u, 2, axis=-1)
    u = a * jax.nn.sigmoid(gt)
    u = lax.conv_general_dilated(
        u, w_dw[:, None, :].astype(u.dtype), window_strides=(1,),
        padding=[(CONV_WIDTH - 1, 0)],
        dimension_numbers=('NWC', 'WIO', 'NWC'),
        feature_group_count=D_MODEL) + b_dw
    u = jax.nn.silu(layer_norm(u, ln_g, ln_b))
    return u @ w_out + b_out


def pool_mixer(h, w, b, scale):
    B_, S_, _ = h.shape
    hg = h.astype(jnp.float32).reshape(B_, S_, POOL_GROUPS, POOL_GROUP_DIM)
    c0 = jnp.pad(jnp.cumsum(hg, axis=1), ((0, 0), (1, 0), (0, 0), (0, 0)))
    t = jnp.arange(S_)
    outs = []
    for gi, win in enumerate(POOL_WINDOWS):
        cg = c0[:, :, gi]
        lo = jnp.maximum(t + 1 - win, 0)
        wsum = cg[:, 1:] - cg[:, lo]
        cnt = jnp.minimum(t + 1, win).astype(jnp.float32)
        outs.append(wsum / cnt[None, :, None] - hg[:, :, gi])
    d = jnp.stack(outs, axis=2).astype(h.dtype)
    y = jnp.einsum('bsgc,gcd->bsgd', d, w).reshape(B_, S_, D_MODEL) + b
    return y * scale


def nsa(h, w_q, w_kv_cmp, w_kv_slc, w_kv_win, cmp_pe, cmp_w1, cmp_w2, w_gate, b_gate, w_o):
    B_, S_, _ = h.shape
    G, HG, DH, QC = NSA_KV_GROUPS, NSA_HPG, NSA_HEAD_DIM, NSA_Q_CHUNK
    q = (h @ w_q).reshape(B_, S_, G, HG, DH)

    def kv(w):
        u = (h @ w).reshape(B_, S_, 2, G, DH)
        return u[:, :, 0], u[:, :, 1]

    kc, vc = kv(w_kv_cmp)
    ks, vs = kv(w_kv_slc)
    kw, vw = kv(w_kv_win)
    gates = jax.nn.sigmoid(h @ w_gate + b_gate).reshape(B_, S_, G, HG, 3)

    n_sub = S_ // CMP_STRIDE
    n_cmp = n_sub - 1

    def compress(x, pe, w1, w2):
        sub = x.reshape(B_, n_sub, CMP_STRIDE, G, DH)
        blk = jnp.concatenate([sub[:, :-1], sub[:, 1:]], axis=2)
        hid = jax.nn.silu(jnp.einsum('bjpgc,pcd->bjgd', blk + pe[:, None, :], w1))
        return hid @ w2

    k_cmp = compress(kc, cmp_pe[0], cmp_w1[0], cmp_w2[0])
    v_cmp = compress(vc, cmp_pe[1], cmp_w1[1], cmp_w2[1])
    cmp_end = jnp.arange(n_cmp) * CMP_STRIDE + CMP_BLOCK - 1

    n_sel = S_ // SEL_BLOCK
    ks_blk = jnp.moveaxis(ks.reshape(B_, n_sel, SEL_BLOCK, G, DH), 3, 1)
    vs_blk = jnp.moveaxis(vs.reshape(B_, n_sel, SEL_BLOCK, G, DH), 3, 1)
    n_topk = min(SEL_TOPK, n_sel)
    sub_per_sel = SEL_BLOCK // CMP_STRIDE
    blk_id = jnp.arange(n_sel)
    bi = jnp.arange(B_)[:, None, None, None]
    gi = jnp.arange(G)[None, :, None, None]

    kw_pad = jnp.pad(kw, ((0, 0), (WINDOW, 0), (0, 0), (0, 0)))
    vw_pad = jnp.pad(vw, ((0, 0), (WINDOW, 0), (0, 0), (0, 0)))
    scale = DH ** -0.5

    def chunk(s0):
        t = s0 + jnp.arange(QC)
        qb = lax.dynamic_slice_in_dim(q, s0, QC, axis=1)
        sc = jnp.einsum('bqghd,bjgd->bghqj', qb, k_cmp) * scale
        p_cmp = masked_softmax(sc, cmp_end[None, :] <= t[:, None])
        o_cmp = jnp.einsum('bghqj,bjgd->bqghd', p_cmp.astype(v_cmp.dtype), v_cmp)
        pg = jnp.pad(jnp.sum(p_cmp, axis=2), ((0, 0), (0, 0), (0, 0), (0, 1)))
        sub_score = pg + jnp.pad(pg[..., :-1], ((0, 0), (0, 0), (0, 0), (1, 0)))
        imp = sub_score.reshape(B_, G, QC, n_sel, sub_per_sel).sum(-1)
        cur = t // SEL_BLOCK
        forced = ((blk_id[None, :] == 0) | (blk_id[None, :] == cur[:, None])
                  | (blk_id[None, :] == cur[:, None] - 1))
        future = blk_id[None, :] > cur[:, None]
        imp = jnp.where(forced, FORCE, jnp.where(future, NEG, imp))
        _, idx = lax.top_k(imp, n_topk)
        ksel = ks_blk[bi, gi, idx]
        vsel = vs_blk[bi, gi, idx]
        kpos = idx[..., None] * SEL_BLOCK + jnp.arange(SEL_BLOCK)
        ss = jnp.einsum('bqghd,bgqnld->bghqnl', qb, ksel) * scale
        smask = (kpos <= t[:, None, None]).reshape(B_, G, 1, QC, -1)
        ps = masked_softmax(ss.reshape(B_, G, HG, QC, -1), smask)
        o_sel = jnp.einsum('bghqm,bgqmd->bqghd', ps.astype(vsel.dtype),
                           vsel.reshape(B_, G, QC, -1, DH))
        kwin = lax.dynamic_slice_in_dim(kw_pad, s0, QC + WINDOW, axis=1)
        vwin = lax.dynamic_slice_in_dim(vw_pad, s0, QC + WINDOW, axis=1)
        wpos = s0 - WINDOW + jnp.arange(QC + WINDOW)
        sw = jnp.einsum('bqghd,bkgd->bghqk', qb, kwin) * scale
        wmask = ((wpos[None, :] <= t[:, None]) & (wpos[None, :] > t[:, None] - WINDOW)
                 & (wpos[None, :] >= 0))
        pw = masked_softmax(sw, wmask)
        o_win = jnp.einsum('bghqk,bkgd->bqghd', pw.astype(vwin.dtype), vwin)
        gb = lax.dynamic_slice_in_dim(gates, s0, QC, axis=1)
        o = gb[..., 0:1] * o_cmp + gb[..., 1:2] * o_sel + gb[..., 2:3] * o_win
        return o.reshape(B_, QC, G * HG * DH)

    o = lax.map(chunk, jnp.arange(S_ // QC) * QC)
    o = jnp.moveaxis(o, 0, 1).reshape(B_, S_, NSA_HEADS * DH)
    return o @ w_o


def mla(h, w_dq, q_norm_g, w_uq, w_dkv, kv_norm_g, w_uk, w_uv, w_o):
    B_, S_, _ = h.shape
    H, QC = MLA_HEADS, MLA_Q_CHUNK
    pos = jnp.arange(S_, dtype=jnp.float32)
    inv_freq = ROPE_THETA ** (-jnp.arange(0, MLA_ROPE_DIM, 2, dtype=jnp.float32) / MLA_ROPE_DIM)
    ang = pos[:, None] * inv_freq[None, :]
    cos, sin = jnp.cos(ang), jnp.sin(ang)
    cq = rms_norm(h @ w_dq, q_norm_g)
    qf = (cq @ w_uq).reshape(B_, S_, H, MLA_NOPE_DIM + MLA_ROPE_DIM)
    q_nope = qf[..., :MLA_NOPE_DIM]
    q_rope = apply_rope(qf[..., MLA_NOPE_DIM:], cos[:, None, :], sin[:, None, :])
    kv = h @ w_dkv
    ckv = rms_norm(kv[..., :MLA_KV_RANK], kv_norm_g)
    k_rope = apply_rope(kv[..., MLA_KV_RANK:], cos, sin)
    k_nope = (ckv @ w_uk).reshape(B_, S_, H, MLA_NOPE_DIM)
    v = (ckv @ w_uv).reshape(B_, S_, H, MLA_V_DIM)
    scale = (MLA_NOPE_DIM + MLA_ROPE_DIM) ** -0.5
    kpos = jnp.arange(S_)

    def block(s0):
        qn = lax.dynamic_slice_in_dim(q_nope, s0, QC, axis=1)
        qr = lax.dynamic_slice_in_dim(q_rope, s0, QC, axis=1)
        s = (jnp.einsum('bqhd,bkhd->bhqk', qn, k_nope)
             + jnp.einsum('bqhr,bkr->bhqk', qr, k_rope)) * scale
        mask = kpos[None, :] <= (s0 + jnp.arange(QC))[:, None]
        p = masked_softmax(s, mask).astype(v.dtype)
        return jnp.einsum('bhqk,bkhd->bqhd', p, v)

    o = lax.map(block, jnp.arange(S_ // QC) * QC)
    o = jnp.moveaxis(o, 0, 1).reshape(B_, S_, H * MLA_V_DIM)
    return o @ w_o


def setup_inputs(seed: int = 0) -> dict:
    key = jax.random.key(seed)
    it = iter(jax.random.split(key, 48))
    f32 = jnp.float32
    NA, NB, NC, ND = N_USES
    D, H, DH, G = D_MODEL, NSA_HEADS, NSA_HEAD_DIM, NSA_KV_GROUPS

    def w(shape, fan_in, gain=1.0):
        return jax.random.normal(next(it), shape, f32) * (gain * fan_in ** -0.5)

    def small(shape, s=0.02):
        return jax.random.normal(next(it), shape, f32) * s

    def gain(shape):
        return 1.0 + jax.random.normal(next(it), shape, f32) * 0.05

    return {
        "x": jax.random.normal(next(it), (BATCH, SEQ, D), f32),
        "c": jax.random.normal(next(it), (BATCH, D), f32),
        "ada_w": w((DEPTH, D, 6 * D), D, 0.5),
        "ada_b": small((DEPTH, 6 * D)),
        "norm1_g": gain((DEPTH, D)),
        "norm2_g": gain((DEPTH, D)),
        "mlp_w1": w((DEPTH, D, D_FF), D),
        "mlp_w2": w((DEPTH, D_FF, D), D_FF),
        "final_g": gain((D,)),
        "conv_w_in": w((NA, D, 2 * D), D),
        "conv_b_in": small((NA, 2 * D)),
        "conv_w_dw": w((NA, CONV_WIDTH, D), CONV_WIDTH),
        "conv_b_dw": small((NA, D)),
        "conv_ln_g": gain((NA, D)),
        "conv_ln_b": small((NA, D)),
        "conv_w_out": w((NA, D, D), D),
        "conv_b_out": small((NA, D)),
        "nsa_w_q": w((NB, D, H * DH), D),
        "nsa_w_kv_cmp": w((NB, D, 2 * G * DH), D),
        "nsa_w_kv_slc": w((NB, D, 2 * G * DH), D),
        "nsa_w_kv_win": w((NB, D, 2 * G * DH), D),
        "nsa_cmp_pe": small((NB, 2, CMP_BLOCK, DH), 0.1),
        "nsa_cmp_w1": w((NB, 2, CMP_BLOCK, DH, DH), CMP_BLOCK * DH),
        "nsa_cmp_w2": w((NB, 2, DH, DH), DH),
        "nsa_w_gate": w((NB, D, 3 * H), D),
        "nsa_b_gate": small((NB, 3 * H)),
        "nsa_w_o": w((NB, H * DH, D), H * DH),
        "pool_w": w((NC, POOL_GROUPS, POOL_GROUP_DIM, POOL_GROUP_DIM), POOL_GROUP_DIM),
        "pool_b": small((NC, D)),
        "pool_scale": 1.0 + small((NC, D), 0.1),
        "mla_w_dq": w((ND, D, MLA_Q_RANK), D),
        "mla_q_norm_g": gain((ND, MLA_Q_RANK)),
        "mla_w_uq": w((ND, MLA_Q_RANK, MLA_HEADS * (MLA_NOPE_DIM + MLA_ROPE_DIM)), MLA_Q_RANK),
        "mla_w_dkv": w((ND, D, MLA_KV_RANK + MLA_ROPE_DIM), D),
        "mla_kv_norm_g": gain((ND, MLA_KV_RANK)),
        "mla_w_uk": w((ND, MLA_KV_RANK, MLA_HEADS * MLA_NOPE_DIM), MLA_KV_RANK),
        "mla_w_uv": w((ND, MLA_KV_RANK, MLA_HEADS * MLA_V_DIM), MLA_KV_RANK),
        "mla_w_o": w((ND, MLA_HEADS * MLA_V_DIM, D), MLA_HEADS * MLA_V_DIM),
    }


def reference(x, c, ada_w, ada_b, norm1_g, norm2_g, mlp_w1, mlp_w2, final_g,
              conv_w_in, conv_b_in, conv_w_dw, conv_b_dw, conv_ln_g, conv_ln_b, conv_w_out, conv_b_out,
              nsa_w_q, nsa_w_kv_cmp, nsa_w_kv_slc, nsa_w_kv_win, nsa_cmp_pe, nsa_cmp_w1, nsa_cmp_w2,
              nsa_w_gate, nsa_b_gate, nsa_w_o,
              pool_w, pool_b, pool_scale,
              mla_w_dq, mla_q_norm_g, mla_w_uq, mla_w_dkv, mla_kv_norm_g, mla_w_uk, mla_w_uv, mla_w_o):
    cs = jax.nn.silu(c)
    for i in range(DEPTH):
        kind, u = i % N_MIXERS, i // N_MIXERS
        mod = cs @ ada_w[i] + ada_b[i]
        sh1, sc1, g1, sh2, sc2, g2 = [m[:, None, :] for m in jnp.split(mod, 6, axis=-1)]
        h = rms_norm(x, norm1_g[i]) * (1 + sc1) + sh1
        if kind == 0:
            y = conformer_conv(h, conv_w_in[u], conv_b_in[u], conv_w_dw[u], conv_b_dw[u],
                               conv_ln_g[u], conv_ln_b[u], conv_w_out[u], conv_b_out[u])
        elif kind == 1:
            y = nsa(h, nsa_w_q[u], nsa_w_kv_cmp[u], nsa_w_kv_slc[u], nsa_w_kv_win[u],
                    nsa_cmp_pe[u], nsa_cmp_w1[u], nsa_cmp_w2[u], nsa_w_gate[u], nsa_b_gate[u], nsa_w_o[u])
        elif kind == 2:
            y = pool_mixer(h, pool_w[u], pool_b[u], pool_scale[u])
        else:
            y = mla(h, mla_w_dq[u], mla_q_norm_g[u], mla_w_uq[u], mla_w_dkv[u], mla_kv_norm_g[u],
                    mla_w_uk[u], mla_w_uv[u], mla_w_o[u])
        x = x + g1 * y
        h = rms_norm(x, norm2_g[i]) * (1 + sc2) + sh2
        x = x + g2 * sq_relu_mlp(h, mlp_w1[i], mlp_w2[i])
    return rms_norm(x, final_g)
```

```python
import functools

import jax
import jax.numpy as jnp
from jax import lax
from jax.experimental import pallas as pl
from jax.experimental.pallas import tpu as pltpu

F32 = jnp.float32
BF16 = jnp.bfloat16

EPS = 1e-6
NEG = -1e30
FORCE = 1e30

CONV_WIDTH = 31
CONV_HALO = 32
NSA_HEADS = 16
NSA_HEAD_DIM = 128
NSA_KV_GROUPS = 4
NSA_HPG = NSA_HEADS // NSA_KV_GROUPS
CMP_BLOCK = 32
CMP_STRIDE = 16
SEL_BLOCK = 64
SEL_TOPK = 16
WINDOW = 512
POOL_WINDOWS = (2, 4, 8, 16)
POOL_HALO = 16
MLA_HEADS = 16
MLA_NOPE_DIM = 128
MLA_ROPE_DIM = 64
MLA_V_DIM = 128
MLA_Q_RANK = 512
MLA_KV_RANK = 256
ROPE_THETA = 10000.0

VMEM_LIMIT_BYTES = 56 * 1024 * 1024


def _params(*semantics):
    return pltpu.CompilerParams(dimension_semantics=semantics, vmem_limit_bytes=VMEM_LIMIT_BYTES)


def _dot(a, b):
    return jnp.dot(a, b, preferred_element_type=F32)


def _dot_nt(a, b):
    return lax.dot_general(a, b, (((1,), (1,)), ((), ())), preferred_element_type=F32)


def _rms(x, g):
    return x * lax.rsqrt(jnp.mean(x * x, axis=-1, keepdims=True) + EPS) * g


def _norm_mod(x, g, sc, sh):
    return _rms(x, g) * (1.0 + sc) + sh


def _sigmoid(x):
    return 1.0 / (1.0 + jnp.exp(-x))


def _iota(shape, dim):
    return lax.broadcasted_iota(jnp.int32, shape, dim)


def _mod_spec(k, tiles_per_seq, width, tiled):
    if tiled:
        return pl.BlockSpec((None, 1, width), lambda i, j: ((i // tiles_per_seq) * 6 + k, 0, j))
    return pl.BlockSpec((None, 1, width), lambda i, *_: ((i // tiles_per_seq) * 6 + k, 0, 0))


def _row_spec(width, tiled):
    if tiled:
        return pl.BlockSpec((1, width), lambda i, j: (0, j))
    return pl.BlockSpec((1, width), lambda i, *_: (0, 0))


def _ada_kernel(c_ref, w_ref, b_ref, o_ref):
    c = c_ref[...]
    cs = c * _sigmoid(c)
    o_ref[...] = _dot(cs.astype(BF16), w_ref[...].astype(BF16)) + b_ref[...]


def _ada_mod(c, ada_w, ada_b):
    depth, d, n = ada_w.shape
    b = c.shape[0]
    bp = 8
    tn = 1024
    cp = jnp.pad(c, ((0, bp - b), (0, 0)))
    out = pl.pallas_call(
        _ada_kernel,
        out_shape=jax.ShapeDtypeStruct((depth, bp, n), F32),
        grid=(depth, n // tn),
        in_specs=[pl.BlockSpec((bp, d), lambda l, j: (0, 0)),
                  pl.BlockSpec((None, d, tn), lambda l, j: (l, 0, j)),
                  pl.BlockSpec((None, 1, tn), lambda l, j: (l, 0, j))],
        out_specs=pl.BlockSpec((None, bp, tn), lambda l, j: (l, 0, j)),
        compiler_params=_params("parallel", "parallel"),
        name="ada_mod",
    )(cp, ada_w, ada_b.reshape(depth, 1, n))
    return out[:, :b]


def _mlp_kernel(x_ref, g_ref, sc_ref, sh_ref, gate_ref, w1_ref, w2_ref, *rest, final):
    if final:
        fg_ref, o_ref, h_ref, acc_ref = rest
    else:
        o_ref, h_ref, acc_ref = rest
    f = pl.program_id(1)

    @pl.when(f == 0)
    def _():
        h_ref[...] = _norm_mod(x_ref[...], g_ref[...], sc_ref[...], sh_ref[...]).astype(BF16)

    a = _dot(h_ref[...], w1_ref[...])
    a = jnp.square(jnp.maximum(a, 0.0)).astype(BF16)
    y = _dot(a, w2_ref[...])

    @pl.when(f == 0)
    def _():
        acc_ref[...] = y

    @pl.when(f > 0)
    def _():
        acc_ref[...] += y

    @pl.when(f == pl.num_programs(1) - 1)
    def _():
        out = x_ref[...] + gate_ref[...] * acc_ref[...]
        if final:
            out = _rms(out, fg_ref[...])
        o_ref[...] = out


def _mlp(x2, mod, norm_g, w1, w2, seq, final_g=None):
    m, d = x2.shape
    dff = w1.shape[1]
    tm, tf = 512, 512
    tps = seq // tm
    final = final_g is not None
    in_specs = [pl.BlockSpec((tm, d), lambda i, f: (i, 0)),
                _row_spec(d, False),
                _mod_spec(4, tps, d, False), _mod_spec(3, tps, d, False), _mod_spec(5, tps, d, False),
                pl.BlockSpec((d, tf), lambda i, f: (0, f)),
                pl.BlockSpec((tf, d), lambda i, f: (f, 0))]
    args = [x2, norm_g.reshape(1, d), mod, mod, mod, w1, w2]
    if final:
        in_specs.append(_row_spec(d, False))
        args.append(final_g.reshape(1, d))
    return pl.pallas_call(
        functools.partial(_mlp_kernel, final=final),
        out_shape=jax.ShapeDtypeStruct((m, d), F32),
        grid=(m // tm, dff // tf),
        in_specs=in_specs,
        out_specs=pl.BlockSpec((tm, d), lambda i, f: (i, 0)),
        scratch_shapes=[pltpu.VMEM((tm, d), BF16), pltpu.VMEM((tm, d), F32)],
        compiler_params=_params("parallel", "arbitrary"),
        name="mlp",
    )(*args)


def _nm_mm_kernel(x_ref, g_ref, sc_ref, sh_ref, w_ref, b_ref, o_ref, h_ref, *, sig_from, tn):
    j = pl.program_id(1)

    @pl.when(j == 0)
    def _():
        h_ref[...] = _norm_mod(x_ref[...], g_ref[...], sc_ref[...], sh_ref[...]).astype(BF16)

    y = _dot(h_ref[...], w_ref[...]) + b_ref[...]
    if sig_from is not None:
        col = j * tn + _iota(y.shape, 1)
        y = jnp.where(col >= sig_from, _sigmoid(y), y)
    o_ref[...] = y.astype(o_ref.dtype)


def _nm_mm(x2, mod, norm_g, w, bias, seq, out_dtype, tn, sig_from=None):
    m, d = x2.shape
    n = w.shape[1]
    tm = 512
    tps = seq // tm
    return pl.pallas_call(
        functools.partial(_nm_mm_kernel, sig_from=sig_from, tn=tn),
        out_shape=jax.ShapeDtypeStruct((m, n), out_dtype),
        grid=(m // tm, n // tn),
        in_specs=[pl.BlockSpec((tm, d), lambda i, j: (i, 0)),
                  _row_spec(d, False),
                  _mod_spec(1, tps, d, False), _mod_spec(0, tps, d, False),
                  pl.BlockSpec((d, tn), lambda i, j: (0, j)),
                  _row_spec(tn, True)],
        out_specs=pl.BlockSpec((tm, tn), lambda i, j: (i, j)),
        scratch_shapes=[pltpu.VMEM((tm, d), BF16)],
        compiler_params=_params("parallel", "arbitrary"),
        name="norm_mod_matmul",
    )(x2, norm_g.reshape(1, d), mod, mod, w, bias.reshape(1, n))


def _mm_res_kernel(a_ref, w_ref, x_ref, gate_ref, o_ref):
    o_ref[...] = x_ref[...] + gate_ref[...] * _dot(a_ref[...], w_ref[...])


def _mm_res(a, w, x2, mod, seq):
    m, k = a.shape
    n = w.shape[1]
    tm, tn = 512, 512
    tps = seq // tm
    return pl.pallas_call(
        _mm_res_kernel,
        out_shape=jax.ShapeDtypeStruct((m, n), F32),
        grid=(m // tm, n // tn),
        in_specs=[pl.BlockSpec((tm, k), lambda i, j: (i, 0)),
                  pl.BlockSpec((k, tn), lambda i, j: (0, j)),
                  pl.BlockSpec((tm, tn), lambda i, j: (i, j)),
                  _mod_spec(2, tps, tn, True)],
        out_specs=pl.BlockSpec((tm, tn), lambda i, j: (i, j)),
        compiler_params=_params("parallel", "arbitrary"),
        name="matmul_residual",
    )(a, w, x2, mod)


def _glu_kernel(x_ref, g_ref, sc_ref, sh_ref, wa_ref, wg_ref, ba_ref, bg_ref, o_ref, h_ref):
    @pl.when(pl.program_id(1) == 0)
    def _():
        h_ref[...] = _norm_mod(x_ref[...], g_ref[...], sc_ref[...], sh_ref[...]).astype(BF16)

    h = h_ref[...]
    a = _dot(h, wa_ref[...]) + ba_ref[...]
    gt = _dot(h, wg_ref[...]) + bg_ref[...]
    o_ref[...] = a * _sigmoid(gt)


def _conv_kernel(u_ref, uh_ref, wdw_ref, bdw_ref, lng_ref, lnb_ref, wo_ref, bo_ref, x_ref, gate_ref,
                 o_ref, ubuf, shifted, v_ref, *, tm, tiles_per_seq):
    i = pl.program_id(0)
    d = u_ref.shape[1]
    rc = 16
    span = tm + CONV_HALO - 8

    @pl.when(pl.program_id(1) == 0)
    def _():
        first = (i % tiles_per_seq) == 0
        ubuf[0:CONV_HALO, :] = jnp.where(first, 0.0, uh_ref[...])
        ubuf[CONV_HALO:, :] = u_ref[...]
        for r in range(1, 8):
            shifted[r - 1] = ubuf[r:r + span, :]

        def chunk(c, carry):
            r0 = pl.multiple_of(c * rc, rc)
            acc = jnp.broadcast_to(bdw_ref[...], (rc, d))
            for k in range(CONV_WIDTH):
                off = CONV_HALO - (CONV_WIDTH - 1) + k
                q, r = divmod(off, 8)
                start = pl.multiple_of(r0 + 8 * q, 8)
                if r == 0:
                    rows = ubuf[pl.ds(start, rc), :]
                else:
                    rows = shifted[r - 1, pl.ds(start, rc), :]
                acc = acc + wdw_ref[k:k + 1, :] * rows
            mu = jnp.mean(acc, axis=-1, keepdims=True)
            cen = acc - mu
            var = jnp.mean(cen * cen, axis=-1, keepdims=True)
            y = cen * lax.rsqrt(var + EPS) * lng_ref[...] + lnb_ref[...]
            v_ref[pl.ds(r0, rc), :] = (y * _sigmoid(y)).astype(BF16)
            return carry

        lax.fori_loop(0, tm // rc, chunk, 0)

    y = _dot(v_ref[...], wo_ref[...]) + bo_ref[...]
    o_ref[...] = x_ref[...] + gate_ref[...] * y


def _conv_layer(x2, mod, norm_g, p, seq):
    m, d = x2.shape
    tm, tn = 512, 512
    tps = seq // tm
    w_in = p["w_in"].astype(BF16)
    u = pl.pallas_call(
        _glu_kernel,
        out_shape=jax.ShapeDtypeStruct((m, d), F32),
        grid=(m // tm, d // tn),
        in_specs=[pl.BlockSpec((tm, d), lambda i, j: (i, 0)),
                  _row_spec(d, False),
                  _mod_spec(1, tps, d, False), _mod_spec(0, tps, d, False),
                  pl.BlockSpec((d, tn), lambda i, j: (0, j)),
                  pl.BlockSpec((d, tn), lambda i, j: (0, j + d // tn)),
                  pl.BlockSpec((1, tn), lambda i, j: (0, j)),
                  pl.BlockSpec((1, tn), lambda i, j: (0, j + d // tn))],
        out_specs=pl.BlockSpec((tm, tn), lambda i, j: (i, j)),
        scratch_shapes=[pltpu.VMEM((tm, d), BF16)],
        compiler_params=_params("parallel", "arbitrary"),
        name="conv_glu",
    )(x2, norm_g.reshape(1, d), mod, mod, w_in, w_in, p["b_in"].reshape(1, 2 * d), p["b_in"].reshape(1, 2 * d))

    tm = 256
    tps = seq // tm
    hb = tm // CONV_HALO
    return pl.pallas_call(
        functools.partial(_conv_kernel, tm=tm, tiles_per_seq=tps),
        out_shape=jax.ShapeDtypeStruct((m, d), F32),
        grid=(m // tm, d // tn),
        in_specs=[pl.BlockSpec((tm, d), lambda i, j: (i, 0)),
                  pl.BlockSpec((CONV_HALO, d), lambda i, j: (jnp.maximum(i * hb - 1, 0), 0)),
                  pl.BlockSpec((CONV_WIDTH, d), lambda i, j: (0, 0)),
                  _row_spec(d, False), _row_spec(d, False), _row_spec(d, False),
                  pl.BlockSpec((d, tn), lambda i, j: (0, j)),
                  _row_spec(tn, True),
                  pl.BlockSpec((tm, tn), lambda i, j: (i, j)),
                  _mod_spec(2, tps, tn, True)],
        out_specs=pl.BlockSpec((tm, tn), lambda i, j: (i, j)),
        scratch_shapes=[pltpu.VMEM((tm + CONV_HALO, d), F32),
                        pltpu.VMEM((7, tm + CONV_HALO - 8, d), F32),
                        pltpu.VMEM((tm, d), BF16)],
        compiler_params=_params("parallel", "arbitrary"),
        name="conv_ln_out",
    )(u, u, p["w_dw"], p["b_dw"].reshape(1, d), p["ln_g"].reshape(1, d), p["ln_b"].reshape(1, d),
      p["w_out"].astype(BF16), p["b_out"].reshape(1, d), x2, mod)


def _pool_kernel(x_ref, xh_ref, g_ref, sc_ref, sh_ref, gate_ref, w_ref, b_ref, scale_ref, o_ref, hbuf,
                 *, tm, tiles_per_seq):
    i = pl.program_id(0)
    d = x_ref.shape[1]
    gd = d // len(POOL_WINDOWS)
    first = (i % tiles_per_seq) == 0
    g, sc, sh = g_ref[...], sc_ref[...], sh_ref[...]
    hbuf[0:POOL_HALO, :] = jnp.where(first, 0.0, _norm_mod(xh_ref[...], g, sc, sh))
    hbuf[POOL_HALO:, :] = _norm_mod(x_ref[...], g, sc, sh)
    tpos = (i % tiles_per_seq) * tm + _iota((tm, 1), 0)
    for gi, win in enumerate(POOL_WINDOWS):
        lo, hi = gi * gd, (gi + 1) * gd
        h = hbuf[POOL_HALO:, lo:hi]
        wsum = h
        for k in range(1, win):
            wsum = wsum + hbuf[POOL_HALO - k:POOL_HALO - k + tm, lo:hi]
        cnt = jnp.minimum(tpos + 1, win).astype(F32)
        dlt = wsum / cnt - h
        y = (_dot(dlt.astype(BF16), w_ref[gi]) + b_ref[:, lo:hi]) * scale_ref[:, lo:hi]
        o_ref[:, lo:hi] = x_ref[:, lo:hi] + gate_ref[:, lo:hi] * y


def _pool_layer(x2, mod, norm_g, p, seq):
    m, d = x2.shape
    tm = 256
    tps = seq // tm
    hb = tm // POOL_HALO
    ng, gd = p["w"].shape[0], p["w"].shape[1]
    return pl.pallas_call(
        functools.partial(_pool_kernel, tm=tm, tiles_per_seq=tps),
        out_shape=jax.ShapeDtypeStruct((m, d), F32),
        grid=(m // tm,),
        in_specs=[pl.BlockSpec((tm, d), lambda i: (i, 0)),
                  pl.BlockSpec((POOL_HALO, d), lambda i: (jnp.maximum(i * hb - 1, 0), 0)),
                  _row_spec(d, False),
                  _mod_spec(1, tps, d, False), _mod_spec(0, tps, d, False), _mod_spec(2, tps, d, False),
                  pl.BlockSpec((ng, gd, gd), lambda i: (0, 0, 0)),
                  _row_spec(d, False), _row_spec(d, False)],
        out_specs=pl.BlockSpec((tm, d), lambda i: (i, 0)),
        scratch_shapes=[pltpu.VMEM((tm + POOL_HALO, d), F32)],
        compiler_params=_params("parallel"),
        name="pool_mixer",
    )(x2, x2, norm_g.reshape(1, d), mod, mod, mod, p["w"].astype(BF16),
      p["b"].reshape(1, d), p["scale"].reshape(1, d))


def _cmp_kernel(x_ref, pe_ref, w1_ref, w2_ref, o_ref, *, n_sub):
    dh = x_ref.shape[1]
    acc_a = jnp.zeros((n_sub, dh), F32)
    acc_b = jnp.zeros((n_sub, dh), F32)
    for p in range(CMP_STRIDE):
        xp = x_ref[pl.ds(p, n_sub, stride=CMP_STRIDE), :]
        acc_a = acc_a + _dot((xp + pe_ref[p:p + 1, :]).astype(BF16), w1_ref[p])
        q = CMP_STRIDE + p
        acc_b = acc_b + _dot((xp + pe_ref[q:q + 1, :]).astype(BF16), w1_ref[q])
    hid = acc_a + pltpu.roll(acc_b, n_sub - 1, 0)
    hid = hid * _sigmoid(hid)
    o_ref[...] = _dot(hid.astype(BF16), w2_ref[...])


def _nsa_compress(kvc3, pe, w1, w2):
    b, s, _ = kvc3.shape
    g, dh = NSA_KV_GROUPS, NSA_HEAD_DIM
    n_sub = s // CMP_STRIDE
    return pl.pallas_call(
        functools.partial(_cmp_kernel, n_sub=n_sub),
        out_shape=jax.ShapeDtypeStruct((2, b, g, n_sub, dh), F32),
        grid=(2, b, g),
        in_specs=[pl.BlockSpec((None, s, dh), lambda kv, bi, gi: (bi, 0, kv * g + gi)),
                  pl.BlockSpec((None, CMP_BLOCK, dh), lambda kv, bi, gi: (kv, 0, 0)),
                  pl.BlockSpec((None, CMP_BLOCK, dh, dh), lambda kv, bi, gi: (kv, 0, 0, 0)),
                  pl.BlockSpec((None, dh, dh), lambda kv, bi, gi: (kv, 0, 0))],
        out_specs=pl.BlockSpec((None, None, None, n_sub, dh), lambda kv, bi, gi: (kv, bi, gi, 0, 0)),
        compiler_params=_params("parallel", "parallel", "parallel"),
        name="nsa_compress",
    )(kvc3, pe, w1.astype(BF16), w2.astype(BF16))


def _sel_block_of(pos):
    assert SEL_BLOCK & (SEL_BLOCK - 1) == 0
    return jnp.right_shift(pos, SEL_BLOCK.bit_length() - 1)


def _split3(x):
    hi = x.astype(BF16)
    r = x - hi.astype(F32)
    mid = r.astype(BF16)
    lo = (r - mid.astype(F32)).astype(BF16)
    return hi, mid, lo


def _nsa_kernel(q_ref, ks_ref, vs_ref, kw_ref, vw_ref, kc_ref, vc_ref, gt_ref, at_ref, o_ref, *, tq, tk):
    hpg, dh = NSA_HPG, NSA_HEAD_DIM
    n_cmp = kc_ref.shape[0]
    n_sel = at_ref.shape[0]
    q0 = pl.program_id(2) * tq
    scale = dh ** -0.5
    q = q_ref[...]
    q4 = jnp.concatenate([q[:, h * dh:(h + 1) * dh] for h in range(hpg)], axis=0)
    tpos = q0 + _iota((tq, 1), 0)

    def softmax_stats(s3, mask):
        s3 = jnp.where(mask[None], s3, NEG)
        mx = jnp.max(s3, axis=-1, keepdims=True)
        e = jnp.where(mask[None], jnp.exp(s3 - mx), 0.0)
        return mx, e, jnp.sum(e, axis=-1, keepdims=True)

    sc = (_dot_nt(q4, kc_ref[...].astype(BF16)) * scale).reshape(hpg, tq, n_cmp)
    cmask = (_iota((tq, n_cmp), 1) * CMP_STRIDE + (CMP_BLOCK - 1)) <= tpos
    _, e, l = softmax_stats(sc, cmask)
    p_cmp = e / jnp.maximum(l, 1e-30)
    o_cmp = _dot(p_cmp.reshape(hpg * tq, n_cmp).astype(BF16), vc_ref[...].astype(BF16))
    pg = p_cmp[0]
    for h in range(1, hpg):
        pg = pg + p_cmp[h]

    at = at_ref[...]
    imp_t = sum(_dot_nt(at, piece) for piece in _split3(pg))
    blk = _iota((n_sel, tq), 0)
    cur = _sel_block_of(q0 + _iota((n_sel, tq), 1))
    forced = (blk == 0) | (blk == cur) | (blk == cur - 1)
    val = jnp.where(forced, FORCE, jnp.where(blk > cur, NEG, imp_t))
    rank = jnp.zeros((n_sel, tq), F32)
    for mm in range(n_sel):
        vm = val[mm:mm + 1, :]
        wins_tie = jnp.where(blk > mm, 1.0, 0.0)
        rank = rank + jnp.where(vm > val, 1.0, jnp.where(vm == val, wins_tie, 0.0))
    sel = jnp.where(rank < SEL_TOPK, 1.0, 0.0).T.astype(BF16)

    def sel_step(j, carry):
        m_i, l_i, acc = carry
        k0 = pl.multiple_of(j * tk, tk)
        s3 = (_dot_nt(q4, ks_ref[pl.ds(k0, tk), :]) * scale).reshape(hpg, tq, tk)
        kpos = k0 + _iota((tq, tk), 1)
        expand = jnp.where(_iota((n_sel, tk), 0) == _sel_block_of(k0 + _iota((n_sel, tk), 1)), 1.0, 0.0)
        chosen = _dot(sel, expand.astype(BF16))
        mask = jnp.where(kpos <= tpos, chosen, 0.0) > 0.5
        s3 = jnp.where(mask[None], s3, NEG)
        m_new = jnp.maximum(m_i, jnp.max(s3, axis=-1, keepdims=True))
        alpha = jnp.exp(m_i - m_new)
        e = jnp.where(mask[None], jnp.exp(s3 - m_new), 0.0)
        l_new = alpha * l_i + jnp.sum(e, axis=-1, keepdims=True)
        pv = _dot(e.reshape(hpg * tq, tk).astype(BF16), vs_ref[pl.ds(k0, tk), :])
        return m_new, l_new, alpha * acc + pv.reshape(hpg, tq, dh)

    init = (jnp.full((hpg, tq, 1), NEG, F32), jnp.zeros((hpg, tq, 1), F32), jnp.zeros((hpg, tq, dh), F32))
    n_kv = (q0 + tq + tk - 1) // tk
    _, l_s, acc_s = lax.fori_loop(0, n_kv, sel_step, init)
    o_sel = acc_s / jnp.maximum(l_s, 1e-30)

    band = WINDOW + tq
    k0w = pl.multiple_of(jnp.maximum(q0 - WINDOW, 0), tq)
    sw = (_dot_nt(q4, kw_ref[pl.ds(k0w, band), :]) * scale).reshape(hpg, tq, band)
    wpos = k0w + _iota((tq, band), 1)
    wmask = jnp.where(wpos <= tpos, wpos, tpos - WINDOW) > tpos - WINDOW
    _, e, l = softmax_stats(sw, wmask)
    pv = _dot(e.reshape(hpg * tq, band).astype(BF16), vw_ref[pl.ds(k0w, band), :])
    o_win = pv.reshape(hpg, tq, dh) / jnp.maximum(l, 1e-30)

    gt = gt_ref[...]
    o_cmp = o_cmp.reshape(hpg, tq, dh)
    outs = []
    for h in range(hpg):
        g_cmp, g_sel, g_win = (gt[:, 3 * h + r:3 * h + r + 1] for r in range(3))
        outs.append(g_cmp * o_cmp[h] + g_sel * o_sel[h] + g_win * o_win[h])
    o_ref[...] = jnp.concatenate(outs, axis=-1).astype(o_ref.dtype)


def _nsa_attention(proj, kv_cmp, gates, seq):
    b = proj.shape[0]
    g, hpg, dh = NSA_KV_GROUPS, NSA_HPG, NSA_HEAD_DIM
    tq, tk = 128, 512
    n_cmp = kv_cmp.shape[3]
    n_sel = seq // SEL_BLOCK
    d_q = g * hpg * dh
    col_q, col_slc, col_win = 0, d_q // dh, (d_q + 2 * g * dh) // dh

    sub = jnp.arange(n_cmp)[None, :] // (SEL_BLOCK // CMP_STRIDE)
    nxt = (jnp.arange(n_cmp)[None, :] + 1) // (SEL_BLOCK // CMP_STRIDE)
    rows = jnp.arange(n_sel)[:, None]
    agg = ((sub == rows).astype(F32) + (nxt == rows).astype(F32)).astype(BF16)

    kv_spec = lambda col: pl.BlockSpec((None, seq, dh), lambda bi, gi, i: (bi, 0, col + gi))
    cmp_spec = lambda kv: pl.BlockSpec((None, None, None, n_cmp, dh), lambda bi, gi, i: (kv, bi, gi, 0, 0))
    return pl.pallas_call(
        functools.partial(_nsa_kernel, tq=tq, tk=tk),
        out_shape=jax.ShapeDtypeStruct((b, seq, d_q), BF16),
        grid=(b, g, seq // tq),
        in_specs=[pl.BlockSpec((None, tq, hpg * dh), lambda bi, gi, i: (bi, i, col_q + gi)),
                  kv_spec(col_slc), kv_spec(col_slc + g), kv_spec(col_win), kv_spec(col_win + g),
                  cmp_spec(0), cmp_spec(1),
                  pl.BlockSpec((None, None, tq, 3 * hpg), lambda bi, gi, i: (bi, gi, i, 0)),
                  pl.BlockSpec((n_sel, n_cmp), lambda bi, gi, i: (0, 0))],
        out_specs=pl.BlockSpec((None, tq, hpg * dh), lambda bi, gi, i: (bi, i, gi)),
        compiler_params=_params("parallel", "parallel", "arbitrary"),
        name="nsa_attention",
    )(proj, proj, proj, proj, proj, kv_cmp, kv_cmp, gates, agg)


def _nsa_layer(x2, mod, norm_g, p, batch, seq):
    m, d = x2.shape
    g, hpg, dh = NSA_KV_GROUPS, NSA_HPG, NSA_HEAD_DIM
    w_a = jnp.concatenate([p["w_q"], p["w_kv_slc"], p["w_kv_win"]], axis=1).astype(BF16)
    n_gate = p["w_gate"].shape[1]
    gate_pad = 128 - n_gate
    w_b = jnp.concatenate([p["w_kv_cmp"], jnp.pad(p["w_gate"], ((0, 0), (0, gate_pad)))], axis=1).astype(BF16)
    n_cmp_cols = p["w_kv_cmp"].shape[1]
    b_b = jnp.concatenate([jnp.zeros((n_cmp_cols,), F32), jnp.pad(p["b_gate"], (0, gate_pad))])
    proj = _nm_mm(x2, mod, norm_g, w_a, jnp.zeros((w_a.shape[1],), F32), seq, BF16, tn=512)
    proj_b = _nm_mm(x2, mod, norm_g, w_b, b_b, seq, F32, tn=384, sig_from=n_cmp_cols)
    proj_b3 = proj_b.reshape(batch, seq, proj_b.shape[1])
    kv_cmp = _nsa_compress(proj_b3, p["cmp_pe"], p["cmp_w1"], p["cmp_w2"])
    gates = proj_b3[:, :, n_cmp_cols:n_cmp_cols + n_gate].reshape(batch, seq, g, 3 * hpg)
    gates = jnp.transpose(gates, (0, 2, 1, 3))
    o = _nsa_attention(proj.reshape(batch, seq, proj.shape[1]), kv_cmp, gates, seq)
    return _mm_res(o.reshape(m, d), p["w_o"].astype(BF16), x2, mod, seq)


def _rope(x, cos2, sin2):
    half = x.shape[1] // 2
    swapped = jnp.concatenate([x[:, half:], x[:, :half]], axis=-1)
    return x * cos2 + swapped * sin2


def _mla_down_kernel(x_ref, g_ref, sc_ref, sh_ref, w_ref, qg_ref, kvg_ref, cos_ref, sin_ref,
                     cq_ref, ckv_ref, kr_ref):
    h = _norm_mod(x_ref[...], g_ref[...], sc_ref[...], sh_ref[...]).astype(BF16)
    y = _dot(h, w_ref[...])
    r0, r1, r2 = MLA_Q_RANK, MLA_Q_RANK + MLA_KV_RANK, MLA_Q_RANK + MLA_KV_RANK + MLA_ROPE_DIM
    cq_ref[...] = _rms(y[:, :r0], qg_ref[...]).astype(BF16)
    ckv_ref[...] = _rms(y[:, r0:r1], kvg_ref[...]).astype(BF16)
    kr_ref[...] = _rope(y[:, r1:r2], cos_ref[...], sin_ref[...]).astype(BF16)


def _mla_q_up_kernel(cq_ref, w_ref, cos_ref, sin_ref, o_ref):
    y = _dot(cq_ref[...], w_ref[...])
    qr = _rope(y[:, MLA_NOPE_DIM:], cos_ref[...], sin_ref[...])
    o_ref[...] = jnp.concatenate([y[:, :MLA_NOPE_DIM], qr], axis=-1).astype(BF16)


def _mla_kv_up_kernel(ckv_ref, kr_ref, wk_ref, wv_ref, k_ref, v_ref):
    ckv = ckv_ref[...]
    k_ref[...] = jnp.concatenate([_dot(ckv, wk_ref[...]).astype(BF16), kr_ref[...]], axis=-1)
    v_ref[...] = _dot(ckv, wv_ref[...]).astype(BF16)


def _mla_attn_kernel(q_ref, k_ref, v_ref, o_ref, *, tq, tk):
    q0 = pl.program_id(2) * tq
    dqk = q_ref.shape[1]
    dv = v_ref.shape[1]
    scale = dqk ** -0.5
    q = q_ref[...]
    tpos = q0 + _iota((tq, 1), 0)

    def step(j, carry):
        m_i, l_i, acc = carry
        k0 = pl.multiple_of(j * tk, tk)
        s = _dot_nt(q, k_ref[pl.ds(k0, tk), :]) * scale
        mask = (k0 + _iota((tq, tk), 1)) <= tpos
        s = jnp.where(mask, s, NEG)
        m_new = jnp.maximum(m_i, jnp.max(s, axis=-1, keepdims=True))
        alpha = jnp.exp(m_i - m_new)
        e = jnp.where(mask, jnp.exp(s - m_new), 0.0)
        l_new = alpha * l_i + jnp.sum(e, axis=-1, keepdims=True)
        return m_new, l_new, alpha * acc + _dot(e.astype(BF16), v_ref[pl.ds(k0, tk), :])

    init = (jnp.full((tq, 1), NEG, F32), jnp.zeros((tq, 1), F32), jnp.zeros((tq, dv), F32))
    _, l, acc = lax.fori_loop(0, (q0 + tq + tk - 1) // tk, step, init)
    o_ref[...] = (acc / jnp.maximum(l, 1e-30)).astype(o_ref.dtype)


def _mla_layer(x2, mod, norm_g, p, batch, seq):
    m, d = x2.shape
    hh, dn, dr, dv = MLA_HEADS, MLA_NOPE_DIM, MLA_ROPE_DIM, MLA_V_DIM
    rq, rkv = MLA_Q_RANK, MLA_KV_RANK
    tm = 512
    tps = seq // tm

    pos = jnp.arange(seq, dtype=F32)
    inv_freq = ROPE_THETA ** (-jnp.arange(0, dr, 2, dtype=F32) / dr)
    ang = pos[:, None] * inv_freq[None, :]
    cos, sin = jnp.cos(ang), jnp.sin(ang)
    cos2 = jnp.concatenate([cos, cos], axis=-1)
    sin2 = jnp.concatenate([-sin, sin], axis=-1)
    rope_spec = pl.BlockSpec((tm, dr), lambda i, *_: (i % tps, 0))

    n_down = rq + rkv + dr
    n_pad = -n_down % 128
    w_down = jnp.pad(jnp.concatenate([p["w_dq"], p["w_dkv"]], axis=1), ((0, 0), (0, n_pad))).astype(BF16)
    cq, ckv, kr = pl.pallas_call(
        _mla_down_kernel,
        out_shape=(jax.ShapeDtypeStruct((m, rq), BF16), jax.ShapeDtypeStruct((m, rkv), BF16),
                   jax.ShapeDtypeStruct((m, dr), BF16)),
        grid=(m // tm,),
        in_specs=[pl.BlockSpec((tm, d), lambda i: (i, 0)),
                  _row_spec(d, False),
                  _mod_spec(1, tps, d, False), _mod_spec(0, tps, d, False),
                  pl.BlockSpec((d, n_down + n_pad), lambda i: (0, 0)),
                  _row_spec(rq, False), _row_spec(rkv, False), rope_spec, rope_spec],
        out_specs=(pl.BlockSpec((tm, rq), lambda i: (i, 0)), pl.BlockSpec((tm, rkv), lambda i: (i, 0)),
                   pl.BlockSpec((tm, dr), lambda i: (i, 0))),
        compiler_params=_params("parallel"),
        name="mla_down",
    )(x2, norm_g.reshape(1, d), mod, mod, w_down, p["q_norm_g"].reshape(1, rq),
      p["kv_norm_g"].reshape(1, rkv), cos2, sin2)

    dqk = dn + dr
    w_uq = jnp.transpose(p["w_uq"].reshape(rq, hh, dqk), (1, 0, 2)).astype(BF16)
    head_out = lambda width: pl.BlockSpec((None, None, tm, width), lambda i, h: (i // tps, h, i % tps, 0))
    qf = pl.pallas_call(
        _mla_q_up_kernel,
        out_shape=jax.ShapeDtypeStruct((batch, hh, seq, dqk), BF16),
        grid=(m // tm, hh),
        in_specs=[pl.BlockSpec((tm, rq), lambda i, h: (i, 0)),
                  pl.BlockSpec((None, rq, dqk), lambda i, h: (h, 0, 0)),
                  rope_spec, rope_spec],
        out_specs=head_out(dqk),
        compiler_params=_params("parallel", "arbitrary"),
        name="mla_q_up",
    )(cq, w_uq, cos2, sin2)

    w_uk = jnp.transpose(p["w_uk"].reshape(rkv, hh, dn), (1, 0, 2)).astype(BF16)
    w_uv = jnp.transpose(p["w_uv"].reshape(rkv, hh, dv), (1, 0, 2)).astype(BF16)
    kf, vf = pl.pallas_call(
        _mla_kv_up_kernel,
        out_shape=(jax.ShapeDtypeStruct((batch, hh, seq, dqk), BF16),
                   jax.ShapeDtypeStruct((batch, hh, seq, dv), BF16)),
        grid=(m // tm, hh),
        in_specs=[pl.BlockSpec((tm, rkv), lambda i, h: (i, 0)),
                  pl.BlockSpec((tm, dr), lambda i, h: (i, 0)),
                  pl.BlockSpec((None, rkv, dn), lambda i, h: (h, 0, 0)),
                  pl.BlockSpec((None, rkv, dv), lambda i, h: (h, 0, 0))],
        out_specs=(head_out(dqk), head_out(dv)),
        compiler_params=_params("parallel", "arbitrary"),
        name="mla_kv_up",
    )(ckv, kr, w_uk, w_uv)

    tq = tk = 512
    o = pl.pallas_call(
        functools.partial(_mla_attn_kernel, tq=tq, tk=tk),
        out_shape=jax.ShapeDtypeStruct((batch, seq, hh * dv), BF16),
        grid=(batch, hh, seq // tq),
        in_specs=[pl.BlockSpec((None, None, tq, dqk), lambda b, h, i: (b, h, i, 0)),
                  pl.BlockSpec((None, None, seq, dqk), lambda b, h, i: (b, h, 0, 0)),
                  pl.BlockSpec((None, None, seq, dv), lambda b, h, i: (b, h, 0, 0))],
        out_specs=pl.BlockSpec((None, tq, dv), lambda b, h, i: (b, i, h)),
        compiler_params=_params("parallel", "parallel", "arbitrary"),
        name="mla_attention",
    )(qf, kf, vf)
    return _mm_res(o.reshape(m, hh * dv), p["w_o"].astype(BF16), x2, mod, seq)


def kernel(x, c, ada_w, ada_b, norm1_g, norm2_g, mlp_w1, mlp_w2, final_g, conv_w_in, conv_b_in, conv_w_dw, conv_b_dw, conv_ln_g, conv_ln_b, conv_w_out, conv_b_out, nsa_w_q, nsa_w_kv_cmp, nsa_w_kv_slc, nsa_w_kv_win, nsa_cmp_pe, nsa_cmp_w1, nsa_cmp_w2, nsa_w_gate, nsa_b_gate, nsa_w_o, pool_w, pool_b, pool_scale, mla_w_dq, mla_q_norm_g, mla_w_uq, mla_w_dkv, mla_kv_norm_g, mla_w_uk, mla_w_uv, mla_w_o):
    batch, seq, d = x.shape
    depth = ada_w.shape[0]
    n_mixers = 4
    mods = _ada_mod(c, ada_w, ada_b)
    x2 = x.reshape(batch * seq, d)
    for i in range(depth):
        kind, u = i % n_mixers, i // n_mixers
        mod = mods[i].reshape(batch * 6, 1, d)
        if kind == 0:
            p = dict(w_in=conv_w_in[u], b_in=conv_b_in[u], w_dw=conv_w_dw[u], b_dw=conv_b_dw[u],
                     ln_g=conv_ln_g[u], ln_b=conv_ln_b[u], w_out=conv_w_out[u], b_out=conv_b_out[u])
            x2 = _conv_layer(x2, mod, norm1_g[i], p, seq)
        elif kind == 1:
            p = dict(w_q=nsa_w_q[u], w_kv_cmp=nsa_w_kv_cmp[u], w_kv_slc=nsa_w_kv_slc[u],
                     w_kv_win=nsa_w_kv_win[u], cmp_pe=nsa_cmp_pe[u], cmp_w1=nsa_cmp_w1[u],
                     cmp_w2=nsa_cmp_w2[u], w_gate=nsa_w_gate[u], b_gate=nsa_b_gate[u], w_o=nsa_w_o[u])
            x2 = _nsa_layer(x2, mod, norm1_g[i], p, batch, seq)
        elif kind == 2:
            p = dict(w=pool_w[u], b=pool_b[u], scale=pool_scale[u])
            x2 = _pool_layer(x2, mod, norm1_g[i], p, seq)
        else:
            p = dict(w_dq=mla_w_dq[u], q_norm_g=mla_q_norm_g[u], w_uq=mla_w_uq[u], w_dkv=mla_w_dkv[u],
                     kv_norm_g=mla_kv_norm_g[u], w_uk=mla_w_uk[u], w_uv=mla_w_uv[u], w_o=mla_w_o[u])
            x2 = _mla_layer(x2, mod, norm1_g[i], p, batch, seq)
        x2 = _mlp(x2, mod, norm2_g[i], mlp_w1[i].astype(BF16), mlp_w2[i].astype(BF16), seq,
                  final_g=final_g if i == depth - 1 else None)
    return x2.reshape(batch, seq, d)
```

```python
import functools

import jax
import jax.numpy as jnp
from jax import lax
from jax.experimental import pallas as pl
from jax.experimental.pallas import tpu as pltpu

F32 = jnp.float32
BF16 = jnp.bfloat16

EPS = 1e-6
NEG = -1e30
FORCE = 1e30
LOG2E = 1.4426950408889634

CONV_WIDTH = 31
CONV_HALO = 32
NSA_HEADS = 16
NSA_HEAD_DIM = 128
NSA_KV_GROUPS = 4
NSA_HPG = NSA_HEADS // NSA_KV_GROUPS
NSA_GROUPS_PER_STEP = 4
CMP_BLOCK = 32
CMP_STRIDE = 16
SEL_BLOCK = 64
SEL_TOPK = 16
WINDOW = 512
POOL_WINDOWS = (2, 4, 8, 16)
POOL_HALO = 16
MLA_HEADS = 16
MLA_NOPE_DIM = 128
MLA_ROPE_DIM = 64
MLA_V_DIM = 128
MLA_Q_RANK = 512
MLA_KV_RANK = 256
ROPE_THETA = 10000.0

VMEM_LIMIT_BYTES = 56 * 1024 * 1024


def _params(*semantics):
    return pltpu.CompilerParams(dimension_semantics=semantics, vmem_limit_bytes=VMEM_LIMIT_BYTES)


def _dot(a, b):
    return jnp.dot(a, b, preferred_element_type=F32)


def _dot_nt(a, b):
    return lax.dot_general(a, b, (((1,), (1,)), ((), ())), preferred_element_type=F32)


def _rms(x, g):
    return x * lax.rsqrt(jnp.mean(x * x, axis=-1, keepdims=True) + EPS) * g


def _norm_mod(x, g, sc, sh):
    return _rms(x, g) * (1.0 + sc) + sh


def _sigmoid(x):
    return 1.0 / (1.0 + jnp.exp(-x))


def _iota(shape, dim):
    return lax.broadcasted_iota(jnp.int32, shape, dim)


def _mod_spec(k, tiles_per_seq, width, tiled):
    if tiled:
        return pl.BlockSpec((None, 1, width), lambda i, j: ((i // tiles_per_seq) * 6 + k, 0, j))
    return pl.BlockSpec((None, 1, width), lambda i, *_: ((i // tiles_per_seq) * 6 + k, 0, 0))


def _row_spec(width, tiled):
    if tiled:
        return pl.BlockSpec((1, width), lambda i, j: (0, j))
    return pl.BlockSpec((1, width), lambda i, *_: (0, 0))


def _ada_kernel(c_ref, w_ref, b_ref, o_ref):
    c = c_ref[...]
    cs = c * _sigmoid(c)
    o_ref[...] = _dot(cs.astype(BF16), w_ref[...].astype(BF16)) + b_ref[...]


def _ada_mod(c, ada_w, ada_b):
    depth, d, n = ada_w.shape
    b = c.shape[0]
    bp = 8
    tn = 1024
    cp = jnp.pad(c, ((0, bp - b), (0, 0)))
    out = pl.pallas_call(
        _ada_kernel,
        out_shape=jax.ShapeDtypeStruct((depth, bp, n), F32),
        grid=(depth, n // tn),
        in_specs=[pl.BlockSpec((bp, d), lambda l, j: (0, 0)),
                  pl.BlockSpec((None, d, tn), lambda l, j: (l, 0, j)),
                  pl.BlockSpec((None, 1, tn), lambda l, j: (l, 0, j))],
        out_specs=pl.BlockSpec((None, bp, tn), lambda l, j: (l, 0, j)),
        compiler_params=_params("parallel", "parallel"),
        name="ada_mod",
    )(cp, ada_w, ada_b.reshape(depth, 1, n))
    return out[:, :b]


def _mlp_kernel(x_ref, g_ref, sc_ref, sh_ref, gate_ref, w1_ref, w2_ref, *rest, final):
    if final:
        fg_ref, o_ref, h_ref, acc_ref = rest
    else:
        o_ref, h_ref, acc_ref = rest
    f = pl.program_id(1)

    @pl.when(f == 0)
    def _():
        h_ref[...] = _norm_mod(x_ref[...], g_ref[...], sc_ref[...], sh_ref[...]).astype(BF16)
        acc_ref[...] = jnp.zeros_like(acc_ref)

    a = _dot(h_ref[...], w1_ref[...])
    a = jnp.square(jnp.maximum(a, 0.0)).astype(BF16)
    acc_ref[...] += _dot(a, w2_ref[...])

    @pl.when(f == pl.num_programs(1) - 1)
    def _():
        out = x_ref[...] + gate_ref[...] * acc_ref[...]
        if final:
            out = _rms(out, fg_ref[...])
        o_ref[...] = out


def _mlp(x2, mod, norm_g, w1, w2, seq, final_g=None):
    m, d = x2.shape
    dff = w1.shape[1]
    tm, tf = 512, 1024
    tps = seq // tm
    final = final_g is not None
    in_specs = [pl.BlockSpec((tm, d), lambda i, f: (i, 0)),
                _row_spec(d, False),
                _mod_spec(4, tps, d, False), _mod_spec(3, tps, d, False), _mod_spec(5, tps, d, False),
                pl.BlockSpec((d, tf), lambda i, f: (0, f)),
                pl.BlockSpec((tf, d), lambda i, f: (f, 0))]
    args = [x2, norm_g.reshape(1, d), mod, mod, mod, w1, w2]
    if final:
        in_specs.append(_row_spec(d, False))
        args.append(final_g.reshape(1, d))
    return pl.pallas_call(
        functools.partial(_mlp_kernel, final=final),
        out_shape=jax.ShapeDtypeStruct((m, d), F32),
        grid=(m // tm, dff // tf),
        in_specs=in_specs,
        out_specs=pl.BlockSpec((tm, d), lambda i, f: (i, 0)),
        scratch_shapes=[pltpu.VMEM((tm, d), BF16), pltpu.VMEM((tm, d), F32)],
        compiler_params=_params("parallel", "arbitrary"),
        name="mlp",
    )(*args)


def _nm_mm_kernel(x_ref, g_ref, sc_ref, sh_ref, w_ref, b_ref, o_ref, h_ref, *, sig_from, tn):
    j = pl.program_id(1)

    @pl.when(j == 0)
    def _():
        h_ref[...] = _norm_mod(x_ref[...], g_ref[...], sc_ref[...], sh_ref[...]).astype(BF16)

    y = _dot(h_ref[...], w_ref[...]) + b_ref[...]
    if sig_from is not None:
        col = j * tn + _iota(y.shape, 1)
        y = jnp.where(col >= sig_from, _sigmoid(y), y)
    o_ref[...] = y.astype(o_ref.dtype)


def _nm_mm(x2, mod, norm_g, w, bias, seq, out_dtype, tn, sig_from=None):
    m, d = x2.shape
    n = w.shape[1]
    tm = 512
    tps = seq // tm
    return pl.pallas_call(
        functools.partial(_nm_mm_kernel, sig_from=sig_from, tn=tn),
        out_shape=jax.ShapeDtypeStruct((m, n), out_dtype),
        grid=(m // tm, n // tn),
        in_specs=[pl.BlockSpec((tm, d), lambda i, j: (i, 0)),
                  _row_spec(d, False),
                  _mod_spec(1, tps, d, False), _mod_spec(0, tps, d, False),
                  pl.BlockSpec((d, tn), lambda i, j: (0, j)),
                  _row_spec(tn, True)],
        out_specs=pl.BlockSpec((tm, tn), lambda i, j: (i, j)),
        scratch_shapes=[pltpu.VMEM((tm, d), BF16)],
        compiler_params=_params("parallel", "arbitrary"),
        name="norm_mod_matmul",
    )(x2, norm_g.reshape(1, d), mod, mod, w, bias.reshape(1, n))


def _mm_res_kernel(a_ref, w_ref, x_ref, gate_ref, o_ref):
    o_ref[...] = x_ref[...] + gate_ref[...] * _dot(a_ref[...], w_ref[...])


def _mm_res(a, w, x2, mod, seq):
    m, k = a.shape
    n = w.shape[1]
    tm, tn = 512, 512
    tps = seq // tm
    return pl.pallas_call(
        _mm_res_kernel,
        out_shape=jax.ShapeDtypeStruct((m, n), F32),
        grid=(m // tm, n // tn),
        in_specs=[pl.BlockSpec((tm, k), lambda i, j: (i, 0)),
                  pl.BlockSpec((k, tn), lambda i, j: (0, j)),
                  pl.BlockSpec((tm, tn), lambda i, j: (i, j)),
                  _mod_spec(2, tps, tn, True)],
        out_specs=pl.BlockSpec((tm, tn), lambda i, j: (i, j)),
        compiler_params=_params("parallel", "arbitrary"),
        name="matmul_residual",
    )(a, w, x2, mod)


def _glu_kernel(x_ref, g_ref, sc_ref, sh_ref, wa_ref, wg_ref, ba_ref, bg_ref, o_ref, h_ref):
    @pl.when(pl.program_id(1) == 0)
    def _():
        h_ref[...] = _norm_mod(x_ref[...], g_ref[...], sc_ref[...], sh_ref[...]).astype(BF16)

    h = h_ref[...]
    a = _dot(h, wa_ref[...]) + ba_ref[...]
    gt = _dot(h, wg_ref[...]) + bg_ref[...]
    o_ref[...] = a * _sigmoid(gt)


def _conv_kernel(u_ref, uh_ref, wdw_ref, bdw_ref, lng_ref, lnb_ref, wo_ref, bo_ref, x_ref, gate_ref,
                 o_ref, ubuf, shifted, v_ref, *, tm, tiles_per_seq):
    i = pl.program_id(0)
    d = u_ref.shape[1]
    rc = 16
    span = tm + CONV_HALO - 8

    @pl.when(pl.program_id(1) == 0)
    def _():
        first = (i % tiles_per_seq) == 0
        ubuf[0:CONV_HALO, :] = jnp.where(first, 0.0, uh_ref[...])
        ubuf[CONV_HALO:, :] = u_ref[...]
        for r in range(1, 8):
            shifted[r - 1] = ubuf[r:r + span, :]

        def chunk(c, carry):
            r0 = pl.multiple_of(c * rc, rc)
            acc = jnp.broadcast_to(bdw_ref[...], (rc, d))
            for k in range(CONV_WIDTH):
                off = CONV_HALO - (CONV_WIDTH - 1) + k
                q, r = divmod(off, 8)
                start = pl.multiple_of(r0 + 8 * q, 8)
                if r == 0:
                    rows = ubuf[pl.ds(start, rc), :]
                else:
                    rows = shifted[r - 1, pl.ds(start, rc), :]
                acc = acc + wdw_ref[k:k + 1, :] * rows
            mu = jnp.mean(acc, axis=-1, keepdims=True)
            cen = acc - mu
            var = jnp.mean(cen * cen, axis=-1, keepdims=True)
            y = cen * lax.rsqrt(var + EPS) * lng_ref[...] + lnb_ref[...]
            v_ref[pl.ds(r0, rc), :] = (y * _sigmoid(y)).astype(BF16)
            return carry

        lax.fori_loop(0, tm // rc, chunk, 0)

    y = _dot(v_ref[...], wo_ref[...]) + bo_ref[...]
    o_ref[...] = x_ref[...] + gate_ref[...] * y


def _conv_layer(x2, mod, norm_g, p, seq):
    m, d = x2.shape
    tm, tn = 512, 512
    tps = seq // tm
    w_in = p["w_in"].astype(BF16)
    u = pl.pallas_call(
        _glu_kernel,
        out_shape=jax.ShapeDtypeStruct((m, d), F32),
        grid=(m // tm, d // tn),
        in_specs=[pl.BlockSpec((tm, d), lambda i, j: (i, 0)),
                  _row_spec(d, False),
                  _mod_spec(1, tps, d, False), _mod_spec(0, tps, d, False),
                  pl.BlockSpec((d, tn), lambda i, j: (0, j)),
                  pl.BlockSpec((d, tn), lambda i, j: (0, j + d // tn)),
                  pl.BlockSpec((1, tn), lambda i, j: (0, j)),
                  pl.BlockSpec((1, tn), lambda i, j: (0, j + d // tn))],
        out_specs=pl.BlockSpec((tm, tn), lambda i, j: (i, j)),
        scratch_shapes=[pltpu.VMEM((tm, d), BF16)],
        compiler_params=_params("parallel", "arbitrary"),
        name="conv_glu",
    )(x2, norm_g.reshape(1, d), mod, mod, w_in, w_in, p["b_in"].reshape(1, 2 * d), p["b_in"].reshape(1, 2 * d))

    tm = 256
    tps = seq // tm
    hb = tm // CONV_HALO
    return pl.pallas_call(
        functools.partial(_conv_kernel, tm=tm, tiles_per_seq=tps),
        out_shape=jax.ShapeDtypeStruct((m, d), F32),
        grid=(m // tm, d // tn),
        in_specs=[pl.BlockSpec((tm, d), lambda i, j: (i, 0)),
                  pl.BlockSpec((CONV_HALO, d), lambda i, j: (jnp.maximum(i * hb - 1, 0), 0)),
                  pl.BlockSpec((CONV_WIDTH, d), lambda i, j: (0, 0)),
                  _row_spec(d, False), _row_spec(d, False), _row_spec(d, False),
                  pl.BlockSpec((d, tn), lambda i, j: (0, j)),
                  _row_spec(tn, True),
                  pl.BlockSpec((tm, tn), lambda i, j: (i, j)),
                  _mod_spec(2, tps, tn, True)],
        out_specs=pl.BlockSpec((tm, tn), lambda i, j: (i, j)),
        scratch_shapes=[pltpu.VMEM((tm + CONV_HALO, d), F32),
                        pltpu.VMEM((7, tm + CONV_HALO - 8, d), F32),
                        pltpu.VMEM((tm, d), BF16)],
        compiler_params=_params("parallel", "arbitrary"),
        name="conv_ln_out",
    )(u, u, p["w_dw"], p["b_dw"].reshape(1, d), p["ln_g"].reshape(1, d), p["ln_b"].reshape(1, d),
      p["w_out"].astype(BF16), p["b_out"].reshape(1, d), x2, mod)


def _pool_kernel(x_ref, xh_ref, g_ref, sc_ref, sh_ref, gate_ref, w_ref, b_ref, scale_ref, o_ref, hbuf,
                 *, tm, tiles_per_seq):
    i = pl.program_id(0)
    d = x_ref.shape[1]
    gd = d // len(POOL_WINDOWS)
    first = (i % tiles_per_seq) == 0
    g, sc, sh = g_ref[...], sc_ref[...], sh_ref[...]
    hbuf[0:POOL_HALO, :] = jnp.where(first, 0.0, _norm_mod(xh_ref[...], g, sc, sh))
    hbuf[POOL_HALO:, :] = _norm_mod(x_ref[...], g, sc, sh)
    tpos = (i % tiles_per_seq) * tm + _iota((tm, 1), 0)
    for gi, win in enumerate(POOL_WINDOWS):
        lo, hi = gi * gd, (gi + 1) * gd
        h = hbuf[POOL_HALO:, lo:hi]
        wsum = h
        for k in range(1, win):
            wsum = wsum + hbuf[POOL_HALO - k:POOL_HALO - k + tm, lo:hi]
        cnt = jnp.minimum(tpos + 1, win).astype(F32)
        dlt = wsum / cnt - h
        y = (_dot(dlt.astype(BF16), w_ref[gi]) + b_ref[:, lo:hi]) * scale_ref[:, lo:hi]
        o_ref[:, lo:hi] = x_ref[:, lo:hi] + gate_ref[:, lo:hi] * y


def _pool_layer(x2, mod, norm_g, p, seq):
    m, d = x2.shape
    tm = 256
    tps = seq // tm
    hb = tm // POOL_HALO
    ng, gd = p["w"].shape[0], p["w"].shape[1]
    return pl.pallas_call(
        functools.partial(_pool_kernel, tm=tm, tiles_per_seq=tps),
        out_shape=jax.ShapeDtypeStruct((m, d), F32),
        grid=(m // tm,),
        in_specs=[pl.BlockSpec((tm, d), lambda i: (i, 0)),
                  pl.BlockSpec((POOL_HALO, d), lambda i: (jnp.maximum(i * hb - 1, 0), 0)),
                  _row_spec(d, False),
                  _mod_spec(1, tps, d, False), _mod_spec(0, tps, d, False), _mod_spec(2, tps, d, False),
                  pl.BlockSpec((ng, gd, gd), lambda i: (0, 0, 0)),
                  _row_spec(d, False), _row_spec(d, False)],
        out_specs=pl.BlockSpec((tm, d), lambda i: (i, 0)),
        scratch_shapes=[pltpu.VMEM((tm + POOL_HALO, d), F32)],
        compiler_params=_params("parallel"),
        name="pool_mixer",
    )(x2, x2, norm_g.reshape(1, d), mod, mod, mod, p["w"].astype(BF16),
      p["b"].reshape(1, d), p["scale"].reshape(1, d))


def _cmp_kernel(x_ref, pe_ref, w1_ref, w2_ref, o_ref, *, n_sub):
    dh = x_ref.shape[1]
    acc_a = jnp.zeros((n_sub, dh), F32)
    acc_b = jnp.zeros((n_sub, dh), F32)
    for p in range(CMP_STRIDE):
        xp = x_ref[pl.ds(p, n_sub, stride=CMP_STRIDE), :]
        acc_a = acc_a + _dot((xp + pe_ref[p:p + 1, :]).astype(BF16), w1_ref[p])
        q = CMP_STRIDE + p
        acc_b = acc_b + _dot((xp + pe_ref[q:q + 1, :]).astype(BF16), w1_ref[q])
    hid = acc_a + pltpu.roll(acc_b, n_sub - 1, 0)
    hid = hid * _sigmoid(hid)
    o_ref[...] = _dot(hid.astype(BF16), w2_ref[...])


def _nsa_compress(kvc3, pe, w1, w2):
    b, s, _ = kvc3.shape
    g, dh = NSA_KV_GROUPS, NSA_HEAD_DIM
    n_sub = s // CMP_STRIDE
    return pl.pallas_call(
        functools.partial(_cmp_kernel, n_sub=n_sub),
        out_shape=jax.ShapeDtypeStruct((2, b, g, n_sub, dh), F32),
        grid=(2, b, g),
        in_specs=[pl.BlockSpec((None, s, dh), lambda kv, bi, gi: (bi, 0, kv * g + gi)),
                  pl.BlockSpec((None, CMP_BLOCK, dh), lambda kv, bi, gi: (kv, 0, 0)),
                  pl.BlockSpec((None, CMP_BLOCK, dh, dh), lambda kv, bi, gi: (kv, 0, 0, 0)),
                  pl.BlockSpec((None, dh, dh), lambda kv, bi, gi: (kv, 0, 0))],
        out_specs=pl.BlockSpec((None, None, None, n_sub, dh), lambda kv, bi, gi: (kv, bi, gi, 0, 0)),
        compiler_params=_params("parallel", "parallel", "parallel"),
        name="nsa_compress",
    )(kvc3, pe, w1.astype(BF16), w2.astype(BF16))


def _sel_block_of(pos):
    assert SEL_BLOCK & (SEL_BLOCK - 1) == 0
    return jnp.right_shift(pos, SEL_BLOCK.bit_length() - 1)


def _split3(x):
    hi = x.astype(BF16)
    r = x - hi.astype(F32)
    mid = r.astype(BF16)
    lo = (r - mid.astype(F32)).astype(BF16)
    return hi, mid, lo


def _nsa_kernel(q_ref, ks_ref, vs_ref, kw_ref, vw_ref, kc_ref, vc_ref, gt_ref, at_ref, et_ref, wb_ref, o_ref,
                *, tq, tk):
    hpg, dh = NSA_HPG, NSA_HEAD_DIM
    assert tq == dh
    ng, n_cmp, _ = kc_ref.shape
    n_sel = at_ref.shape[0]
    rows = hpg * tq
    q0 = pl.program_id(2) * tq
    c = dh ** -0.5 * LOG2E
    tpos = q0 + _iota((tq, 1), 0)
    gcols = lambda g: slice(g * dh, (g + 1) * dh)

    def select(g):
        q = q_ref[:, g * hpg * dh:(g + 1) * hpg * dh]
        q4 = jnp.concatenate([q[:, h * dh:(h + 1) * dh] for h in range(hpg)], axis=0)
        sc = _dot_nt(q4, kc_ref[g].astype(BF16)).reshape(hpg, tq, n_cmp)
        cvalid = (_iota((tq, n_cmp), 1) * CMP_STRIDE + (CMP_BLOCK - 1)) <= tpos
        sc = sc + jnp.where(cvalid, 0.0, NEG)[None]
        e = jnp.exp2((sc - jnp.max(sc, axis=-1, keepdims=True)) * c)
        l = jnp.sum(e, axis=-1, keepdims=True)
        row_ok = jnp.where(tpos >= CMP_BLOCK - 1, 1.0, 0.0)
        p_cmp = e * (row_ok / jnp.maximum(l, 1e-30))
        o_cmp = _dot(p_cmp.reshape(rows, n_cmp).astype(BF16), vc_ref[g].astype(BF16))
        pg = p_cmp[0]
        for h in range(1, hpg):
            pg = pg + p_cmp[h]

        at = at_ref[...]
        imp_t = sum(_dot_nt(at, piece) for piece in _split3(pg))
        blk = _iota((n_sel, tq), 0)
        cur = _sel_block_of(q0 + _iota((n_sel, tq), 1))
        forced = (blk == 0) | (blk == cur) | (blk == cur - 1)
        val = jnp.where(forced, FORCE, jnp.where(blk > cur, NEG, imp_t))
        ranks = []
        for r0 in range(0, n_sel, 8):
            mine = val[r0:r0 + 8, :]
            rank = jnp.zeros((8, tq), F32)
            for mm in range(n_sel):
                other = val[mm:mm + 1, :]
                if mm < r0:
                    rank = rank + jnp.where(other >= mine, 1.0, 0.0)
                elif mm >= r0 + 8:
                    rank = rank + jnp.where(other > mine, 1.0, 0.0)
                else:
                    wins_tie = jnp.where(_iota((8, tq), 0) + r0 > mm, 1.0, 0.0)
                    rank = rank + jnp.where(other > mine, 1.0, jnp.where(other == mine, wins_tie, 0.0))
            ranks.append(rank)
        rank = jnp.concatenate(ranks, axis=0)
        keep = jnp.where(rank < SEL_TOPK, jnp.where(blk <= cur, 0.0, NEG), NEG)
        keep = jnp.concatenate([keep, jnp.zeros((dh - n_sel, tq), F32)], axis=0).T.astype(BF16)
        q_sel = jnp.concatenate([q4, jnp.concatenate([keep] * hpg, axis=0)], axis=1)
        return o_cmp, q4, q_sel

    o_cmp, q4, q_sel = zip(*[select(g) for g in range(ng)])

    def sel_tile(j, carry, last):
        k0 = pl.multiple_of(j * tk, tk)
        onehot = et_ref[pl.ds(k0, tk), :]
        out = []
        for g in range(ng):
            m_i, acc = carry[g]
            k_aug = jnp.concatenate([ks_ref[pl.ds(k0, tk), gcols(g)], onehot], axis=1)
            s = _dot_nt(q_sel[g], k_aug)
            if last:
                causal = (k0 + _iota((tq, tk), 1)) <= tpos
                s = jnp.where(causal[None], s.reshape(hpg, tq, tk), NEG).reshape(rows, tk)
            m_new = jnp.maximum(m_i, jnp.max(s, axis=-1, keepdims=True))
            alpha = jnp.exp2((m_i - m_new) * c)
            e = jnp.exp2((s - m_new) * c).astype(BF16)
            v_aug = jnp.concatenate([vs_ref[pl.ds(k0, tk), gcols(g)], jnp.ones((tk, dh), BF16)], axis=1)
            out.append((m_new, alpha * acc + _dot(e, v_aug)))
        return tuple(out)

    j_last = q0 // tk
    init = tuple((jnp.full((rows, 1), NEG, F32), jnp.zeros((rows, 2 * dh), F32)) for _ in range(ng))
    carry = lax.fori_loop(0, j_last, lambda j, cr: sel_tile(j, cr, False), init)
    carry = sel_tile(j_last, carry, True)

    band = WINDOW + tq
    k0w = pl.multiple_of(jnp.maximum(q0 - WINDOW, 0), tq)
    eye = jnp.where(_iota((rows, dh), 1) == (_iota((rows, dh), 0) & (tq - 1)), 1.0, 0.0).astype(BF16)
    for g in range(ng):
        acc = carry[g][1]
        o_sel = acc[:, :dh] / jnp.maximum(acc[:, dh:], 1e-30)
        q_win = jnp.concatenate([q4[g], eye], axis=1)
        k_aug = jnp.concatenate([kw_ref[pl.ds(k0w, band), gcols(g)], wb_ref[...]], axis=1)
        sw = _dot_nt(q_win, k_aug)
        e = jnp.exp2((sw - jnp.max(sw, axis=-1, keepdims=True)) * c).astype(BF16)
        v_aug = jnp.concatenate([vw_ref[pl.ds(k0w, band), gcols(g)], jnp.ones((band, dh), BF16)], axis=1)
        pv = _dot(e, v_aug)
        o_win = pv[:, :dh] / jnp.maximum(pv[:, dh:], 1e-30)

        gt = gt_ref[g]
        for h in range(hpg):
            g_cmp, g_sel, g_win = (gt[:, 3 * h + r:3 * h + r + 1] for r in range(3))
            hs = slice(h * tq, (h + 1) * tq)
            o_ref[:, (g * hpg + h) * dh:(g * hpg + h + 1) * dh] = (
                g_cmp * o_cmp[g][hs] + g_sel * o_sel[hs] + g_win * o_win[hs]).astype(o_ref.dtype)


def _nsa_attention(proj, kv_cmp, gates, seq):
    b = proj.shape[0]
    g, hpg, dh = NSA_KV_GROUPS, NSA_HPG, NSA_HEAD_DIM
    tq, tk = 128, 512
    n_cmp = kv_cmp.shape[3]
    n_sel = seq // SEL_BLOCK
    d_q = g * hpg * dh
    col_q, col_slc, col_win = 0, d_q // dh, (d_q + 2 * g * dh) // dh

    sub = jnp.arange(n_cmp)[None, :] // (SEL_BLOCK // CMP_STRIDE)
    nxt = (jnp.arange(n_cmp)[None, :] + 1) // (SEL_BLOCK // CMP_STRIDE)
    rows = jnp.arange(n_sel)[:, None]
    agg = ((sub == rows).astype(F32) + (nxt == rows).astype(F32)).astype(BF16)
    assert n_sel <= dh
    block_onehot = (jnp.arange(seq)[:, None] // SEL_BLOCK == jnp.arange(dh)[None, :]).astype(BF16)
    band = WINDOW + tq
    n_pat = WINDOW // tq + 1
    pat_q0 = jnp.arange(n_pat)[:, None, None] * tq
    key = jnp.maximum(pat_q0 - WINDOW, 0) + jnp.arange(band)[None, :, None]
    qry = pat_q0 + jnp.arange(tq)[None, None, :]
    win_bias = jnp.where((key <= qry) & (key > qry - WINDOW), 0.0, NEG).astype(BF16)

    ng = NSA_GROUPS_PER_STEP
    assert g % ng == 0 and col_slc % ng == 0 and col_win % ng == 0
    kv_spec = lambda col: pl.BlockSpec((None, seq, ng * dh), lambda bi, gi, i: (bi, 0, col // ng + gi))
    cmp_spec = lambda kv: pl.BlockSpec((None, None, ng, n_cmp, dh), lambda bi, gi, i: (kv, bi, gi, 0, 0))
    return pl.pallas_call(
        functools.partial(_nsa_kernel, tq=tq, tk=tk),
        out_shape=jax.ShapeDtypeStruct((b, seq, d_q), BF16),
        grid=(b, g // ng, seq // tq),
        in_specs=[pl.BlockSpec((None, tq, ng * hpg * dh), lambda bi, gi, i: (bi, i, col_q + gi)),
                  kv_spec(col_slc), kv_spec(col_slc + g), kv_spec(col_win), kv_spec(col_win + g),
                  cmp_spec(0), cmp_spec(1),
                  pl.BlockSpec((None, ng, tq, 3 * hpg), lambda bi, gi, i: (bi, gi, i, 0)),
                  pl.BlockSpec((n_sel, n_cmp), lambda bi, gi, i: (0, 0)),
                  pl.BlockSpec((seq, dh), lambda bi, gi, i: (0, 0)),
                  pl.BlockSpec((None, band, tq), lambda bi, gi, i: (jnp.minimum(i, n_pat - 1), 0, 0))],
        out_specs=pl.BlockSpec((None, tq, ng * hpg * dh), lambda bi, gi, i: (bi, i, gi)),
        compiler_params=_params("parallel", "parallel", "arbitrary"),
        name="nsa_attention",
    )(proj, proj, proj, proj, proj, kv_cmp, kv_cmp, gates, agg, block_onehot, win_bias)


def _nsa_layer(x2, mod, norm_g, p, batch, seq):
    m, d = x2.shape
    g, hpg, dh = NSA_KV_GROUPS, NSA_HPG, NSA_HEAD_DIM
    w_a = jnp.concatenate([p["w_q"], p["w_kv_slc"], p["w_kv_win"]], axis=1).astype(BF16)
    n_gate = p["w_gate"].shape[1]
    gate_pad = 128 - n_gate
    w_b = jnp.concatenate([p["w_kv_cmp"], jnp.pad(p["w_gate"], ((0, 0), (0, gate_pad)))], axis=1).astype(BF16)
    n_cmp_cols = p["w_kv_cmp"].shape[1]
    b_b = jnp.concatenate([jnp.zeros((n_cmp_cols,), F32), jnp.pad(p["b_gate"], (0, gate_pad))])
    proj = _nm_mm(x2, mod, norm_g, w_a, jnp.zeros((w_a.shape[1],), F32), seq, BF16, tn=512)
    proj_b = _nm_mm(x2, mod, norm_g, w_b, b_b, seq, F32, tn=384, sig_from=n_cmp_cols)
    proj_b3 = proj_b.reshape(batch, seq, proj_b.shape[1])
    kv_cmp = _nsa_compress(proj_b3, p["cmp_pe"], p["cmp_w1"], p["cmp_w2"])
    gates = proj_b3[:, :, n_cmp_cols:n_cmp_cols + n_gate].reshape(batch, seq, g, 3 * hpg)
    gates = jnp.transpose(gates, (0, 2, 1, 3))
    o = _nsa_attention(proj.reshape(batch, seq, proj.shape[1]), kv_cmp, gates, seq)
    return _mm_res(o.reshape(m, d), p["w_o"].astype(BF16), x2, mod, seq)


def _rope(x, cos2, sin2):
    half = x.shape[1] // 2
    swapped = jnp.concatenate([x[:, half:], x[:, :half]], axis=-1)
    return x * cos2 + swapped * sin2


def _mla_down_kernel(x_ref, g_ref, sc_ref, sh_ref, w_ref, qg_ref, kvg_ref, cos_ref, sin_ref,
                     cq_ref, ckv_ref, kr_ref):
    h = _norm_mod(x_ref[...], g_ref[...], sc_ref[...], sh_ref[...]).astype(BF16)
    y = _dot(h, w_ref[...])
    r0, r1, r2 = MLA_Q_RANK, MLA_Q_RANK + MLA_KV_RANK, MLA_Q_RANK + MLA_KV_RANK + MLA_ROPE_DIM
    cq_ref[...] = _rms(y[:, :r0], qg_ref[...]).astype(BF16)
    ckv_ref[...] = _rms(y[:, r0:r1], kvg_ref[...]).astype(BF16)
    kr_ref[...] = _rope(y[:, r1:r2], cos_ref[...], sin_ref[...]).astype(BF16)


def _mla_q_up_kernel(cq_ref, w_ref, cos_ref, sin_ref, o_ref):
    y = _dot(cq_ref[...], w_ref[...])
    qr = _rope(y[:, MLA_NOPE_DIM:], cos_ref[...], sin_ref[...])
    o_ref[...] = jnp.concatenate([y[:, :MLA_NOPE_DIM], qr], axis=-1).astype(BF16)


def _mla_kv_up_kernel(ckv_ref, kr_ref, wk_ref, wv_ref, k_ref, v_ref):
    ckv = ckv_ref[...]
    k_ref[...] = jnp.concatenate([_dot(ckv, wk_ref[...]).astype(BF16), kr_ref[...]], axis=-1)
    v_ref[...] = _dot(ckv, wv_ref[...]).astype(BF16)


def _mla_attn_kernel(q_ref, k_ref, v_ref, o_ref, *, tq, tk):
    dqk = q_ref.shape[2]
    nh, _, dv = v_ref.shape
    assert tq == tk
    c = dqk ** -0.5 * LOG2E
    ones = jnp.ones((tk, dv), BF16)

    def tile(j, carry, diagonal):
        k0 = pl.multiple_of(j * tk, tk)
        out = []
        for h in range(nh):
            m_i, acc = carry[h]
            s = _dot_nt(q_ref[h], k_ref[h, pl.ds(k0, tk), :])
            if diagonal:
                s = jnp.where(_iota((tq, tk), 1) <= _iota((tq, tk), 0), s, NEG)
            m_new = jnp.maximum(m_i, jnp.max(s, axis=-1, keepdims=True))
            alpha = jnp.exp2((m_i - m_new) * c)
            e = jnp.exp2((s - m_new) * c).astype(BF16)
            v_aug = jnp.concatenate([v_ref[h, pl.ds(k0, tk), :], ones], axis=1)
            out.append((m_new, alpha * acc + _dot(e, v_aug)))
        return tuple(out)

    i = pl.program_id(2)
    init = tuple((jnp.full((tq, 1), NEG, F32), jnp.zeros((tq, 2 * dv), F32)) for _ in range(nh))
    carry = lax.fori_loop(0, i, lambda j, cr: tile(j, cr, False), init)
    carry = tile(i, carry, True)
    for h in range(nh):
        acc = carry[h][1]
        o_ref[:, h * dv:(h + 1) * dv] = (acc[:, :dv] / jnp.maximum(acc[:, dv:], 1e-30)).astype(o_ref.dtype)


def _mla_layer(x2, mod, norm_g, p, batch, seq):
    m, d = x2.shape
    hh, dn, dr, dv = MLA_HEADS, MLA_NOPE_DIM, MLA_ROPE_DIM, MLA_V_DIM
    rq, rkv = MLA_Q_RANK, MLA_KV_RANK
    tm = 512
    tps = seq // tm

    pos = jnp.arange(seq, dtype=F32)
    inv_freq = ROPE_THETA ** (-jnp.arange(0, dr, 2, dtype=F32) / dr)
    ang = pos[:, None] * inv_freq[None, :]
    cos, sin = jnp.cos(ang), jnp.sin(ang)
    cos2 = jnp.concatenate([cos, cos], axis=-1)
    sin2 = jnp.concatenate([-sin, sin], axis=-1)
    rope_spec = pl.BlockSpec((tm, dr), lambda i, *_: (i % tps, 0))

    n_down = rq + rkv + dr
    n_pad = -n_down % 128
    w_down = jnp.pad(jnp.concatenate([p["w_dq"], p["w_dkv"]], axis=1), ((0, 0), (0, n_pad))).astype(BF16)
    cq, ckv, kr = pl.pallas_call(
        _mla_down_kernel,
        out_shape=(jax.ShapeDtypeStruct((m, rq), BF16), jax.ShapeDtypeStruct((m, rkv), BF16),
                   jax.ShapeDtypeStruct((m, dr), BF16)),
        grid=(m // tm,),
        in_specs=[pl.BlockSpec((tm, d), lambda i: (i, 0)),
                  _row_spec(d, False),
                  _mod_spec(1, tps, d, False), _mod_spec(0, tps, d, False),
                  pl.BlockSpec((d, n_down + n_pad), lambda i: (0, 0)),
                  _row_spec(rq, False), _row_spec(rkv, False), rope_spec, rope_spec],
        out_specs=(pl.BlockSpec((tm, rq), lambda i: (i, 0)), pl.BlockSpec((tm, rkv), lambda i: (i, 0)),
                   pl.BlockSpec((tm, dr), lambda i: (i, 0))),
        compiler_params=_params("parallel"),
        name="mla_down",
    )(x2, norm_g.reshape(1, d), mod, mod, w_down, p["q_norm_g"].reshape(1, rq),
      p["kv_norm_g"].reshape(1, rkv), cos2, sin2)

    dqk = dn + dr
    w_uq = jnp.transpose(p["w_uq"].reshape(rq, hh, dqk), (1, 0, 2)).astype(BF16)
    head_out = lambda width: pl.BlockSpec((None, None, tm, width), lambda i, h: (i // tps, h, i % tps, 0))
    qf = pl.pallas_call(
        _mla_q_up_kernel,
        out_shape=jax.ShapeDtypeStruct((batch, hh, seq, dqk), BF16),
        grid=(m // tm, hh),
        in_specs=[pl.BlockSpec((tm, rq), lambda i, h: (i, 0)),
                  pl.BlockSpec((None, rq, dqk), lambda i, h: (h, 0, 0)),
                  rope_spec, rope_spec],
        out_specs=head_out(dqk),
        compiler_params=_params("parallel", "arbitrary"),
        name="mla_q_up",
    )(cq, w_uq, cos2, sin2)

    w_uk = jnp.transpose(p["w_uk"].reshape(rkv, hh, dn), (1, 0, 2)).astype(BF16)
    w_uv = jnp.transpose(p["w_uv"].reshape(rkv, hh, dv), (1, 0, 2)).astype(BF16)
    kf, vf = pl.pallas_call(
        _mla_kv_up_kernel,
        out_shape=(jax.ShapeDtypeStruct((batch, hh, seq, dqk), BF16),
                   jax.ShapeDtypeStruct((batch, hh, seq, dv), BF16)),
        grid=(m // tm, hh),
        in_specs=[pl.BlockSpec((tm, rkv), lambda i, h: (i, 0)),
                  pl.BlockSpec((tm, dr), lambda i, h: (i, 0)),
                  pl.BlockSpec((None, rkv, dn), lambda i, h: (h, 0, 0)),
                  pl.BlockSpec((None, rkv, dv), lambda i, h: (h, 0, 0))],
        out_specs=(head_out(dqk), head_out(dv)),
        compiler_params=_params("parallel", "arbitrary"),
        name="mla_kv_up",
    )(ckv, kr, w_uk, w_uv)

    tq = tk = 512
    nh = 4
    o = pl.pallas_call(
        functools.partial(_mla_attn_kernel, tq=tq, tk=tk),
        out_shape=jax.ShapeDtypeStruct((batch, seq, hh * dv), BF16),
        grid=(batch, hh // nh, seq // tq),
        in_specs=[pl.BlockSpec((None, nh, tq, dqk), lambda b, h, i: (b, h, i, 0)),
                  pl.BlockSpec((None, nh, seq, dqk), lambda b, h, i: (b, h, 0, 0)),
                  pl.BlockSpec((None, nh, seq, dv), lambda b, h, i: (b, h, 0, 0))],
        out_specs=pl.BlockSpec((None, tq, nh * dv), lambda b, h, i: (b, i, h)),
        compiler_params=_params("parallel", "parallel", "arbitrary"),
        name="mla_attention",
    )(qf, kf, vf)
    return _mm_res(o.reshape(m, hh * dv), p["w_o"].astype(BF16), x2, mod, seq)


def kernel(x, c, ada_w, ada_b, norm1_g, norm2_g, mlp_w1, mlp_w2, final_g, conv_w_in, conv_b_in, conv_w_dw, conv_b_dw, conv_ln_g, conv_ln_b, conv_w_out, conv_b_out, nsa_w_q, nsa_w_kv_cmp, nsa_w_kv_slc, nsa_w_kv_win, nsa_cmp_pe, nsa_cmp_w1, nsa_cmp_w2, nsa_w_gate, nsa_b_gate, nsa_w_o, pool_w, pool_b, pool_scale, mla_w_dq, mla_q_norm_g, mla_w_uq, mla_w_dkv, mla_kv_norm_g, mla_w_uk, mla_w_uv, mla_w_o):
    batch, seq, d = x.shape
    depth = ada_w.shape[0]
    n_mixers = 4
    mods = _ada_mod(c, ada_w, ada_b)
    x2 = x.reshape(batch * seq, d)
    for i in range(depth):
        kind, u = i % n_mixers, i // n_mixers
        mod = mods[i].reshape(batch * 6, 1, d)
        if kind == 0:
            p = dict(w_in=conv_w_in[u], b_in=conv_b_in[u], w_dw=conv_w_dw[u], b_dw=conv_b_dw[u],
                     ln_g=conv_ln_g[u], ln_b=conv_ln_b[u], w_out=conv_w_out[u], b_out=conv_b_out[u])
            x2 = _conv_layer(x2, mod, norm1_g[i], p, seq)
        elif kind == 1:
            p = dict(w_q=nsa_w_q[u], w_kv_cmp=nsa_w_kv_cmp[u], w_kv_slc=nsa_w_kv_slc[u],
                     w_kv_win=nsa_w_kv_win[u], cmp_pe=nsa_cmp_pe[u], cmp_w1=nsa_cmp_w1[u],
                     cmp_w2=nsa_cmp_w2[u], w_gate=nsa_w_gate[u], b_gate=nsa_b_gate[u], w_o=nsa_w_o[u])
            x2 = _nsa_layer(x2, mod, norm1_g[i], p, batch, seq)
        elif kind == 2:
            p = dict(w=pool_w[u], b=pool_b[u], scale=pool_scale[u])
            x2 = _pool_layer(x2, mod, norm1_g[i], p, seq)
        else:
            p = dict(w_dq=mla_w_dq[u], q_norm_g=mla_q_norm_g[u], w_uq=mla_w_uq[u], w_dkv=mla_w_dkv[u],
                     kv_norm_g=mla_kv_norm_g[u], w_uk=mla_w_uk[u], w_uv=mla_w_uv[u], w_o=mla_w_o[u])
            x2 = _mla_layer(x2, mod, norm1_g[i], p, batch, seq)
        x2 = _mlp(x2, mod, norm2_g[i], mlp_w1[i].astype(BF16), mlp_w2[i].astype(BF16), seq,
                  final_g=final_g if i == depth - 1 else None)
    return x2.reshape(batch, seq, d)
```

```python
import functools

import jax
import jax.numpy as jnp
from jax import lax
from jax.experimental import pallas as pl
from jax.experimental.pallas import tpu as pltpu

F32 = jnp.float32
BF16 = jnp.bfloat16

EPS = 1e-6
NEG = -1e30
FORCE = 1e30
LOG2E = 1.4426950408889634

CONV_WIDTH = 31
CONV_HALO = 32
NSA_HEADS = 16
NSA_HEAD_DIM = 128
NSA_KV_GROUPS = 4
NSA_HPG = NSA_HEADS // NSA_KV_GROUPS
NSA_GROUPS_PER_STEP = 4
CMP_BLOCK = 32
CMP_STRIDE = 16
SEL_BLOCK = 64
SEL_TOPK = 16
WINDOW = 512
POOL_WINDOWS = (2, 4, 8, 16)
POOL_HALO = 16
MLA_HEADS = 16
MLA_NOPE_DIM = 128
MLA_ROPE_DIM = 64
MLA_V_DIM = 128
MLA_Q_RANK = 512
MLA_KV_RANK = 256
ROPE_THETA = 10000.0

VMEM_LIMIT_BYTES = 56 * 1024 * 1024
LANES = 128


def _params(*semantics):
    return pltpu.CompilerParams(dimension_semantics=semantics, vmem_limit_bytes=VMEM_LIMIT_BYTES)


def _dot(a, b):
    return jnp.dot(a, b, preferred_element_type=F32)


def _dot_nt(a, b):
    return lax.dot_general(a, b, (((1,), (1,)), ((), ())), preferred_element_type=F32)


def _rms(x, g):
    return x * lax.rsqrt(jnp.mean(x * x, axis=-1, keepdims=True) + EPS) * g


def _norm_mod(x, g, sc, sh):
    return _rms(x, g) * (1.0 + sc) + sh


def _sigmoid(x):
    return 1.0 / (1.0 + jnp.exp(-x))


def _iota(shape, dim):
    return lax.broadcasted_iota(jnp.int32, shape, dim)


def _mod_spec(k, tiles_per_seq, width, tiled):
    if tiled:
        return pl.BlockSpec((None, 1, width), lambda i, j: ((i // tiles_per_seq) * 6 + k, 0, j))
    return pl.BlockSpec((None, 1, width), lambda i, *_: ((i // tiles_per_seq) * 6 + k, 0, 0))


def _row_spec(width, tiled):
    if tiled:
        return pl.BlockSpec((1, width), lambda i, j: (0, j))
    return pl.BlockSpec((1, width), lambda i, *_: (0, 0))


def _ada_kernel(c_ref, w_ref, b_ref, o_ref):
    c = c_ref[...]
    cs = c * _sigmoid(c)
    o_ref[...] = _dot(cs.astype(BF16), w_ref[...].astype(BF16)) + b_ref[...]


def _ada_mod(c, ada_w, ada_b):
    depth, d, n = ada_w.shape
    b = c.shape[0]
    bp = 8
    tn = 1024
    cp = jnp.pad(c, ((0, bp - b), (0, 0)))
    out = pl.pallas_call(
        _ada_kernel,
        out_shape=jax.ShapeDtypeStruct((depth, bp, n), F32),
        grid=(depth, n // tn),
        in_specs=[pl.BlockSpec((bp, d), lambda l, j: (0, 0)),
                  pl.BlockSpec((None, d, tn), lambda l, j: (l, 0, j)),
                  pl.BlockSpec((None, 1, tn), lambda l, j: (l, 0, j))],
        out_specs=pl.BlockSpec((None, bp, tn), lambda l, j: (l, 0, j)),
        compiler_params=_params("parallel", "parallel"),
        name="ada_mod",
    )(cp, ada_w, ada_b.reshape(depth, 1, n))
    return out[:, :b]


def _mlp_kernel(x_ref, g_ref, sc_ref, sh_ref, gate_ref, w1_ref, w2_ref, *rest, final):
    if final:
        fg_ref, o_ref, h_ref, acc_ref = rest
    else:
        o_ref, h_ref, acc_ref = rest
    f = pl.program_id(1)

    @pl.when(f == 0)
    def _():
        h_ref[...] = _norm_mod(x_ref[...], g_ref[...], sc_ref[...], sh_ref[...]).astype(BF16)
        acc_ref[...] = jnp.zeros_like(acc_ref)

    a = _dot(h_ref[...], w1_ref[...])
    a = jnp.square(jnp.maximum(a, 0.0)).astype(BF16)
    acc_ref[...] += _dot(a, w2_ref[...])

    @pl.when(f == pl.num_programs(1) - 1)
    def _():
        out = x_ref[...] + gate_ref[...] * acc_ref[...]
        if final:
            out = _rms(out, fg_ref[...])
        o_ref[...] = out


def _mlp(x2, mod, norm_g, w1, w2, layer, seq, final_g=None):
    m, d = x2.shape
    dff = w1.shape[2]
    tm, tf = 512, 1024
    tps = seq // tm
    final = final_g is not None
    in_specs = [pl.BlockSpec((tm, d), lambda i, f: (i, 0)),
                _row_spec(d, False),
                _mod_spec(4, tps, d, False), _mod_spec(3, tps, d, False), _mod_spec(5, tps, d, False),
                pl.BlockSpec((None, d, tf), lambda i, f: (layer, 0, f)),
                pl.BlockSpec((None, tf, d), lambda i, f: (layer, f, 0))]
    args = [x2, norm_g.reshape(1, d), mod, mod, mod, w1, w2]
    if final:
        in_specs.append(_row_spec(d, False))
        args.append(final_g.reshape(1, d))
    return pl.pallas_call(
        functools.partial(_mlp_kernel, final=final),
        out_shape=jax.ShapeDtypeStruct((m, d), F32),
        grid=(m // tm, dff // tf),
        in_specs=in_specs,
        out_specs=pl.BlockSpec((tm, d), lambda i, f: (i, 0)),
        scratch_shapes=[pltpu.VMEM((tm, d), BF16), pltpu.VMEM((tm, d), F32)],
        compiler_params=_params("parallel", "arbitrary"),
        name="mlp",
    )(*args)


def _nm_mm_kernel(x_ref, g_ref, sc_ref, sh_ref, w_ref, b_ref, o_ref, h_ref, *, sig_from, tn):
    j = pl.program_id(1)

    @pl.when(j == 0)
    def _():
        h_ref[...] = _norm_mod(x_ref[...], g_ref[...], sc_ref[...], sh_ref[...]).astype(BF16)

    y = _dot(h_ref[...], w_ref[...]) + b_ref[...]
    if sig_from is not None:
        col = j * tn + _iota(y.shape, 1)
        y = jnp.where(col >= sig_from, _sigmoid(y), y)
    o_ref[...] = y.astype(o_ref.dtype)


def _nm_mm(x2, mod, norm_g, w, bias, seq, out_dtype, tn, sig_from=None):
    m, d = x2.shape
    n = w.shape[1]
    tm = 512
    tps = seq // tm
    return pl.pallas_call(
        functools.partial(_nm_mm_kernel, sig_from=sig_from, tn=tn),
        out_shape=jax.ShapeDtypeStruct((m, n), out_dtype),
        grid=(m // tm, n // tn),
        in_specs=[pl.BlockSpec((tm, d), lambda i, j: (i, 0)),
                  _row_spec(d, False),
                  _mod_spec(1, tps, d, False), _mod_spec(0, tps, d, False),
                  pl.BlockSpec((d, tn), lambda i, j: (0, j)),
                  _row_spec(tn, True)],
        out_specs=pl.BlockSpec((tm, tn), lambda i, j: (i, j)),
        scratch_shapes=[pltpu.VMEM((tm, d), BF16)],
        compiler_params=_params("parallel", "arbitrary"),
        name="norm_mod_matmul",
    )(x2, norm_g.reshape(1, d), mod, mod, w, bias.reshape(1, n))


def _mm_res_kernel(a_ref, w_ref, x_ref, gate_ref, o_ref):
    o_ref[...] = x_ref[...] + gate_ref[...] * _dot(a_ref[...], w_ref[...])


def _mm_res(a, w, x2, mod, seq):
    m, k = a.shape
    n = w.shape[1]
    tm = 512
    tps = seq // tm
    return pl.pallas_call(
        _mm_res_kernel,
        out_shape=jax.ShapeDtypeStruct((m, n), F32),
        grid=(m // tm,),
        in_specs=[pl.BlockSpec((tm, k), lambda i: (i, 0)),
                  pl.BlockSpec((k, n), lambda i: (0, 0)),
                  pl.BlockSpec((tm, n), lambda i: (i, 0)),
                  _mod_spec(2, tps, n, False)],
        out_specs=pl.BlockSpec((tm, n), lambda i: (i, 0)),
        compiler_params=_params("parallel"),
        name="matmul_residual",
    )(a, w, x2, mod)


def _glu_kernel(x_ref, g_ref, sc_ref, sh_ref, wa_ref, wg_ref, ba_ref, bg_ref, o_ref, h_ref):
    @pl.when(pl.program_id(1) == 0)
    def _():
        h_ref[...] = _norm_mod(x_ref[...], g_ref[...], sc_ref[...], sh_ref[...]).astype(BF16)

    h = h_ref[...]
    a = _dot(h, wa_ref[...]) + ba_ref[...]
    gt = _dot(h, wg_ref[...]) + bg_ref[...]
    o_ref[...] = a * _sigmoid(gt)


def _conv_kernel(u_ref, uh_ref, wdw_ref, bdw_ref, lng_ref, lnb_ref, wo_ref, bo_ref, x_ref, gate_ref,
                 o_ref, ubuf, shifted, conv_ref, v_ref, *, tm, tiles_per_seq):
    i = pl.program_id(0)
    d = u_ref.shape[1]
    rc = 64
    rn = 16
    span = tm + CONV_HALO - 8

    @pl.when(pl.program_id(1) == 0)
    def _():
        first = (i % tiles_per_seq) == 0
        ubuf[0:CONV_HALO, :] = jnp.where(first, 0.0, uh_ref[...])
        ubuf[CONV_HALO:, :] = u_ref[...]

        def column(ct, carry):
            cs = pl.ds(pl.multiple_of(ct * LANES, LANES), LANES)
            for r in range(1, 8):
                shifted[r - 1] = ubuf[r:r + span, cs]
            taps = [jnp.broadcast_to(wdw_ref[k:k + 1, cs], (8, LANES)) for k in range(CONV_WIDTH)]
            bias = jnp.broadcast_to(bdw_ref[:, cs], (8, LANES))

            def chunk(c, inner):
                r0 = pl.multiple_of(c * rc, rc)
                for v in range(rc // 8):
                    acc = bias
                    for k in range(CONV_WIDTH):
                        off = CONV_HALO - (CONV_WIDTH - 1) + k
                        q, r = divmod(off, 8)
                        start = pl.multiple_of(r0 + 8 * (q + v), 8)
                        rows = ubuf[pl.ds(start, 8), cs] if r == 0 else shifted[r - 1, pl.ds(start, 8), :]
                        acc = acc + taps[k] * rows
                    conv_ref[pl.ds(pl.multiple_of(r0 + 8 * v, 8), 8), cs] = acc
                return inner

            lax.fori_loop(0, tm // rc, chunk, 0)
            return carry

        lax.fori_loop(0, d // LANES, column, 0)

        def norm_chunk(c, carry):
            r0 = pl.multiple_of(c * rn, rn)
            acc = conv_ref[pl.ds(r0, rn), :]
            mu = jnp.mean(acc, axis=-1, keepdims=True)
            cen = acc - mu
            var = jnp.mean(cen * cen, axis=-1, keepdims=True)
            y = cen * lax.rsqrt(var + EPS) * lng_ref[...] + lnb_ref[...]
            v_ref[pl.ds(r0, rn), :] = (y * _sigmoid(y)).astype(BF16)
            return carry

        lax.fori_loop(0, tm // rn, norm_chunk, 0, unroll=4)

    y = _dot(v_ref[...], wo_ref[...]) + bo_ref[...]
    o_ref[...] = x_ref[...] + gate_ref[...] * y


def _conv_layer(x2, mod, norm_g, p, seq):
    m, d = x2.shape
    tm, tn = 512, 1024
    tps = seq // tm
    w_in = p["w_in"].astype(BF16)
    u = pl.pallas_call(
        _glu_kernel,
        out_shape=jax.ShapeDtypeStruct((m, d), F32),
        grid=(m // tm, d // tn),
        in_specs=[pl.BlockSpec((tm, d), lambda i, j: (i, 0)),
                  _row_spec(d, False),
                  _mod_spec(1, tps, d, False), _mod_spec(0, tps, d, False),
                  pl.BlockSpec((d, tn), lambda i, j: (0, j)),
                  pl.BlockSpec((d, tn), lambda i, j: (0, j + d // tn)),
                  pl.BlockSpec((1, tn), lambda i, j: (0, j)),
                  pl.BlockSpec((1, tn), lambda i, j: (0, j + d // tn))],
        out_specs=pl.BlockSpec((tm, tn), lambda i, j: (i, j)),
        scratch_shapes=[pltpu.VMEM((tm, d), BF16)],
        compiler_params=_params("parallel", "arbitrary"),
        name="conv_glu",
    )(x2, norm_g.reshape(1, d), mod, mod, w_in, w_in, p["b_in"].reshape(1, 2 * d), p["b_in"].reshape(1, 2 * d))

    tm = 512
    tps = seq // tm
    hb = tm // CONV_HALO
    return pl.pallas_call(
        functools.partial(_conv_kernel, tm=tm, tiles_per_seq=tps),
        out_shape=jax.ShapeDtypeStruct((m, d), F32),
        grid=(m // tm, d // tn),
        in_specs=[pl.BlockSpec((tm, d), lambda i, j: (i, 0)),
                  pl.BlockSpec((CONV_HALO, d), lambda i, j: (jnp.maximum(i * hb - 1, 0), 0)),
                  pl.BlockSpec((CONV_WIDTH, d), lambda i, j: (0, 0)),
                  _row_spec(d, False), _row_spec(d, False), _row_spec(d, False),
                  pl.BlockSpec((d, tn), lambda i, j: (0, j)),
                  _row_spec(tn, True),
                  pl.BlockSpec((tm, tn), lambda i, j: (i, j)),
                  _mod_spec(2, tps, tn, True)],
        out_specs=pl.BlockSpec((tm, tn), lambda i, j: (i, j)),
        scratch_shapes=[pltpu.VMEM((tm + CONV_HALO, d), F32),
                        pltpu.VMEM((7, tm + CONV_HALO - 8, LANES), F32),
                        pltpu.VMEM((tm, d), F32),
                        pltpu.VMEM((tm, d), BF16)],
        compiler_params=_params("parallel", "arbitrary"),
        name="conv_ln_out",
    )(u, u, p["w_dw"], p["b_dw"].reshape(1, d), p["ln_g"].reshape(1, d), p["ln_b"].reshape(1, d),
      p["w_out"].astype(BF16), p["b_out"].reshape(1, d), x2, mod)


def _pool_kernel(x_ref, xh_ref, g_ref, sc_ref, sh_ref, gate_ref, w_ref, b_ref, scale_ref, o_ref, hbuf,
                 *, tm, tiles_per_seq):
    i = pl.program_id(0)
    d = x_ref.shape[1]
    gd = d // len(POOL_WINDOWS)
    first = (i % tiles_per_seq) == 0
    g, sc, sh = g_ref[...], sc_ref[...], sh_ref[...]
    hbuf[0:POOL_HALO, :] = jnp.where(first, 0.0, _norm_mod(xh_ref[...], g, sc, sh))
    hbuf[POOL_HALO:, :] = _norm_mod(x_ref[...], g, sc, sh)
    tpos = (i % tiles_per_seq) * tm + _iota((tm, 1), 0)
    for gi, win in enumerate(POOL_WINDOWS):
        lo, hi = gi * gd, (gi + 1) * gd
        h = hbuf[POOL_HALO:, lo:hi]
        wsum = h
        for k in range(1, win):
            wsum = wsum + hbuf[POOL_HALO - k:POOL_HALO - k + tm, lo:hi]
        cnt = jnp.minimum(tpos + 1, win).astype(F32)
        dlt = wsum / cnt - h
        y = (_dot(dlt.astype(BF16), w_ref[gi]) + b_ref[:, lo:hi]) * scale_ref[:, lo:hi]
        o_ref[:, lo:hi] = x_ref[:, lo:hi] + gate_ref[:, lo:hi] * y


def _pool_layer(x2, mod, norm_g, p, seq):
    m, d = x2.shape
    tm = 256
    tps = seq // tm
    hb = tm // POOL_HALO
    ng, gd = p["w"].shape[0], p["w"].shape[1]
    return pl.pallas_call(
        functools.partial(_pool_kernel, tm=tm, tiles_per_seq=tps),
        out_shape=jax.ShapeDtypeStruct((m, d), F32),
        grid=(m // tm,),
        in_specs=[pl.BlockSpec((tm, d), lambda i: (i, 0)),
                  pl.BlockSpec((POOL_HALO, d), lambda i: (jnp.maximum(i * hb - 1, 0), 0)),
                  _row_spec(d, False),
                  _mod_spec(1, tps, d, False), _mod_spec(0, tps, d, False), _mod_spec(2, tps, d, False),
                  pl.BlockSpec((ng, gd, gd), lambda i: (0, 0, 0)),
                  _row_spec(d, False), _row_spec(d, False)],
        out_specs=pl.BlockSpec((tm, d), lambda i: (i, 0)),
        scratch_shapes=[pltpu.VMEM((tm + POOL_HALO, d), F32)],
        compiler_params=_params("parallel"),
        name="pool_mixer",
    )(x2, x2, norm_g.reshape(1, d), mod, mod, mod, p["w"].astype(BF16),
      p["b"].reshape(1, d), p["scale"].reshape(1, d))


def _cmp_kernel(x_ref, pe_ref, w1_ref, w2_ref, o_ref, *, n_sub):
    dh = x_ref.shape[1]
    acc_a = jnp.zeros((n_sub, dh), F32)
    acc_b = jnp.zeros((n_sub, dh), F32)
    for p in range(CMP_STRIDE):
        xp = x_ref[pl.ds(p, n_sub, stride=CMP_STRIDE), :]
        acc_a = acc_a + _dot((xp + pe_ref[p:p + 1, :]).astype(BF16), w1_ref[p])
        q = CMP_STRIDE + p
        acc_b = acc_b + _dot((xp + pe_ref[q:q + 1, :]).astype(BF16), w1_ref[q])
    hid = acc_a + pltpu.roll(acc_b, n_sub - 1, 0)
    hid = hid * _sigmoid(hid)
    o_ref[...] = _dot(hid.astype(BF16), w2_ref[...])


def _nsa_compress(kvc3, pe, w1, w2):
    b, s, _ = kvc3.shape
    g, dh = NSA_KV_GROUPS, NSA_HEAD_DIM
    n_sub = s // CMP_STRIDE
    return pl.pallas_call(
        functools.partial(_cmp_kernel, n_sub=n_sub),
        out_shape=jax.ShapeDtypeStruct((2, b, g, n_sub, dh), F32),
        grid=(2, b, g),
        in_specs=[pl.BlockSpec((None, s, dh), lambda kv, bi, gi: (bi, 0, kv * g + gi)),
                  pl.BlockSpec((None, CMP_BLOCK, dh), lambda kv, bi, gi: (kv, 0, 0)),
                  pl.BlockSpec((None, CMP_BLOCK, dh, dh), lambda kv, bi, gi: (kv, 0, 0, 0)),
                  pl.BlockSpec((None, dh, dh), lambda kv, bi, gi: (kv, 0, 0))],
        out_specs=pl.BlockSpec((None, None, None, n_sub, dh), lambda kv, bi, gi: (kv, bi, gi, 0, 0)),
        compiler_params=_params("parallel", "parallel", "parallel"),
        name="nsa_compress",
    )(kvc3, pe, w1.astype(BF16), w2.astype(BF16))


def _sel_block_of(pos):
    assert SEL_BLOCK & (SEL_BLOCK - 1) == 0
    return jnp.right_shift(pos, SEL_BLOCK.bit_length() - 1)


def _split3(x):
    hi = x.astype(BF16)
    r = x - hi.astype(F32)
    mid = r.astype(BF16)
    lo = (r - mid.astype(F32)).astype(BF16)
    return hi, mid, lo


def _nsa_kernel(q_ref, ks_ref, vs_ref, kw_ref, vw_ref, kc_ref, vc_ref, gt_ref, at_ref, et_ref, wb_ref, o_ref,
                *, tq, tk):
    hpg, dh = NSA_HPG, NSA_HEAD_DIM
    assert tq == dh
    ng, n_cmp, _ = kc_ref.shape
    n_sel = at_ref.shape[0]
    rows = hpg * tq
    q0 = pl.program_id(2) * tq
    c = dh ** -0.5 * LOG2E
    tpos = q0 + _iota((tq, 1), 0)
    gcols = lambda g: slice(g * dh, (g + 1) * dh)

    def select(g):
        q = q_ref[:, g * hpg * dh:(g + 1) * hpg * dh]
        q4 = jnp.concatenate([q[:, h * dh:(h + 1) * dh] for h in range(hpg)], axis=0)
        sc = _dot_nt(q4, kc_ref[g].astype(BF16)).reshape(hpg, tq, n_cmp)
        cvalid = (_iota((tq, n_cmp), 1) * CMP_STRIDE + (CMP_BLOCK - 1)) <= tpos
        sc = sc + jnp.where(cvalid, 0.0, NEG)[None]
        e = jnp.exp2((sc - jnp.max(sc, axis=-1, keepdims=True)) * c)
        l = jnp.sum(e, axis=-1, keepdims=True)
        row_ok = jnp.where(tpos >= CMP_BLOCK - 1, 1.0, 0.0)
        p_cmp = e * (row_ok / jnp.maximum(l, 1e-30))
        o_cmp = _dot(p_cmp.reshape(rows, n_cmp).astype(BF16), vc_ref[g].astype(BF16))
        pg = p_cmp[0]
        for h in range(1, hpg):
            pg = pg + p_cmp[h]

        at = at_ref[...]
        imp_t = sum(_dot_nt(at, piece) for piece in _split3(pg))
        blk = _iota((n_sel, tq), 0)
        cur = _sel_block_of(q0 + _iota((n_sel, tq), 1))
        forced = (blk == 0) | (blk == cur) | (blk == cur - 1)
        val = jnp.where(forced, FORCE, jnp.where(blk > cur, NEG, imp_t))
        ranks = []
        for r0 in range(0, n_sel, 8):
            mine = val[r0:r0 + 8, :]
            rank = jnp.zeros((8, tq), F32)
            for mm in range(n_sel):
                other = val[mm:mm + 1, :]
                if mm < r0:
                    rank = rank + jnp.where(other >= mine, 1.0, 0.0)
                elif mm >= r0 + 8:
                    rank = rank + jnp.where(other > mine, 1.0, 0.0)
                else:
                    wins_tie = jnp.where(_iota((8, tq), 0) + r0 > mm, 1.0, 0.0)
                    rank = rank + jnp.where(other > mine, 1.0, jnp.where(other == mine, wins_tie, 0.0))
            ranks.append(rank)
        rank = jnp.concatenate(ranks, axis=0)
        keep = jnp.where(rank < SEL_TOPK, jnp.where(blk <= cur, 0.0, NEG), NEG)
        keep = jnp.concatenate([keep, jnp.zeros((dh - n_sel, tq), F32)], axis=0).T.astype(BF16)
        q_sel = jnp.concatenate([q4, jnp.concatenate([keep] * hpg, axis=0)], axis=1)
        return o_cmp, q4, q_sel

    o_cmp, q4, q_sel = zip(*[select(g) for g in range(ng)])

    def sel_tile(j, carry, last):
        k0 = pl.multiple_of(j * tk, tk)
        onehot = et_ref[pl.ds(k0, tk), :]
        out = []
        for g in range(ng):
            m_i, acc = carry[g]
            k_aug = jnp.concatenate([ks_ref[pl.ds(k0, tk), gcols(g)], onehot], axis=1)
            s = _dot_nt(q_sel[g], k_aug)
            if last:
                causal = (k0 + _iota((tq, tk), 1)) <= tpos
                s = jnp.where(causal[None], s.reshape(hpg, tq, tk), NEG).reshape(rows, tk)
            m_new = jnp.maximum(m_i, jnp.max(s, axis=-1, keepdims=True))
            alpha = jnp.exp2((m_i - m_new) * c)
            e = jnp.exp2((s - m_new) * c).astype(BF16)
            v_aug = jnp.concatenate([vs_ref[pl.ds(k0, tk), gcols(g)], jnp.ones((tk, dh), BF16)], axis=1)
            out.append((m_new, alpha * acc + _dot(e, v_aug)))
        return tuple(out)

    j_last = q0 // tk
    init = tuple((jnp.full((rows, 1), NEG, F32), jnp.zeros((rows, 2 * dh), F32)) for _ in range(ng))
    carry = lax.fori_loop(0, j_last, lambda j, cr: sel_tile(j, cr, False), init)
    carry = sel_tile(j_last, carry, True)

    band = WINDOW + tq
    k0w = pl.multiple_of(jnp.maximum(q0 - WINDOW, 0), tq)
    eye = jnp.where(_iota((rows, dh), 1) == (_iota((rows, dh), 0) & (tq - 1)), 1.0, 0.0).astype(BF16)
    for g in range(ng):
        acc = carry[g][1]
        o_sel = acc[:, :dh] / jnp.maximum(acc[:, dh:], 1e-30)
        q_win = jnp.concatenate([q4[g], eye], axis=1)
        k_aug = jnp.concatenate([kw_ref[pl.ds(k0w, band), gcols(g)], wb_ref[...]], axis=1)
        sw = _dot_nt(q_win, k_aug)
        e = jnp.exp2((sw - jnp.max(sw, axis=-1, keepdims=True)) * c).astype(BF16)
        v_aug = jnp.concatenate([vw_ref[pl.ds(k0w, band), gcols(g)], jnp.ones((band, dh), BF16)], axis=1)
        pv = _dot(e, v_aug)
        o_win = pv[:, :dh] / jnp.maximum(pv[:, dh:], 1e-30)

        gt = gt_ref[g]
        for h in range(hpg):
            g_cmp, g_sel, g_win = (gt[:, 3 * h + r:3 * h + r + 1] for r in range(3))
            hs = slice(h * tq, (h + 1) * tq)
            o_ref[:, (g * hpg + h) * dh:(g * hpg + h + 1) * dh] = (
                g_cmp * o_cmp[g][hs] + g_sel * o_sel[hs] + g_win * o_win[hs]).astype(o_ref.dtype)


def _nsa_attention(proj, kv_cmp, gates, seq):
    b = proj.shape[0]
    g, hpg, dh = NSA_KV_GROUPS, NSA_HPG, NSA_HEAD_DIM
    tq, tk = 128, 512
    n_cmp = kv_cmp.shape[3]
    n_sel = seq // SEL_BLOCK
    d_q = g * hpg * dh
    col_q, col_slc, col_win = 0, d_q // dh, (d_q + 2 * g * dh) // dh

    sub = jnp.arange(n_cmp)[None, :] // (SEL_BLOCK // CMP_STRIDE)
    nxt = (jnp.arange(n_cmp)[None, :] + 1) // (SEL_BLOCK // CMP_STRIDE)
    rows = jnp.arange(n_sel)[:, None]
    agg = ((sub == rows).astype(F32) + (nxt == rows).astype(F32)).astype(BF16)
    assert n_sel <= dh
    block_onehot = (jnp.arange(seq)[:, None] // SEL_BLOCK == jnp.arange(dh)[None, :]).astype(BF16)
    band = WINDOW + tq
    n_pat = WINDOW // tq + 1
    pat_q0 = jnp.arange(n_pat)[:, None, None] * tq
    key = jnp.maximum(pat_q0 - WINDOW, 0) + jnp.arange(band)[None, :, None]
    qry = pat_q0 + jnp.arange(tq)[None, None, :]
    win_bias = jnp.where((key <= qry) & (key > qry - WINDOW), 0.0, NEG).astype(BF16)

    ng = NSA_GROUPS_PER_STEP
    assert g % ng == 0 and col_slc % ng == 0 and col_win % ng == 0
    kv_spec = lambda col: pl.BlockSpec((None, seq, ng * dh), lambda bi, gi, i: (bi, 0, col // ng + gi))
    cmp_spec = lambda kv: pl.BlockSpec((None, None, ng, n_cmp, dh), lambda bi, gi, i: (kv, bi, gi, 0, 0))
    return pl.pallas_call(
        functools.partial(_nsa_kernel, tq=tq, tk=tk),
        out_shape=jax.ShapeDtypeStruct((b, seq, d_q), BF16),
        grid=(b, g // ng, seq // tq),
        in_specs=[pl.BlockSpec((None, tq, ng * hpg * dh), lambda bi, gi, i: (bi, i, col_q + gi)),
                  kv_spec(col_slc), kv_spec(col_slc + g), kv_spec(col_win), kv_spec(col_win + g),
                  cmp_spec(0), cmp_spec(1),
                  pl.BlockSpec((None, ng, tq, 3 * hpg), lambda bi, gi, i: (bi, gi, i, 0)),
                  pl.BlockSpec((n_sel, n_cmp), lambda bi, gi, i: (0, 0)),
                  pl.BlockSpec((seq, dh), lambda bi, gi, i: (0, 0)),
                  pl.BlockSpec((None, band, tq), lambda bi, gi, i: (jnp.minimum(i, n_pat - 1), 0, 0))],
        out_specs=pl.BlockSpec((None, tq, ng * hpg * dh), lambda bi, gi, i: (bi, i, gi)),
        compiler_params=_params("parallel", "parallel", "arbitrary"),
        name="nsa_attention",
    )(proj, proj, proj, proj, proj, kv_cmp, kv_cmp, gates, agg, block_onehot, win_bias)


def _nsa_layer(x2, mod, norm_g, p, batch, seq):
    m, d = x2.shape
    g, hpg, dh = NSA_KV_GROUPS, NSA_HPG, NSA_HEAD_DIM
    w_a = jnp.concatenate([p["w_q"], p["w_kv_slc"], p["w_kv_win"]], axis=1).astype(BF16)
    n_gate = p["w_gate"].shape[1]
    gate_pad = 128 - n_gate
    w_b = jnp.concatenate([p["w_kv_cmp"], jnp.pad(p["w_gate"], ((0, 0), (0, gate_pad)))], axis=1).astype(BF16)
    n_cmp_cols = p["w_kv_cmp"].shape[1]
    b_b = jnp.concatenate([jnp.zeros((n_cmp_cols,), F32), jnp.pad(p["b_gate"], (0, gate_pad))])
    proj = _nm_mm(x2, mod, norm_g, w_a, jnp.zeros((w_a.shape[1],), F32), seq, BF16, tn=1024)
    proj_b = _nm_mm(x2, mod, norm_g, w_b, b_b, seq, F32, tn=384, sig_from=n_cmp_cols)
    proj_b3 = proj_b.reshape(batch, seq, proj_b.shape[1])
    kv_cmp = _nsa_compress(proj_b3, p["cmp_pe"], p["cmp_w1"], p["cmp_w2"])
    gates = proj_b3[:, :, n_cmp_cols:n_cmp_cols + n_gate].reshape(batch, seq, g, 3 * hpg)
    gates = jnp.transpose(gates, (0, 2, 1, 3))
    o = _nsa_attention(proj.reshape(batch, seq, proj.shape[1]), kv_cmp, gates, seq)
    return _mm_res(o.reshape(m, d), p["w_o"].astype(BF16), x2, mod, seq)


def _rope(x, cos2, sin2):
    half = x.shape[1] // 2
    swapped = jnp.concatenate([x[:, half:], x[:, :half]], axis=-1)
    return x * cos2 + swapped * sin2


def _mla_proj_kernel(x_ref, g_ref, sc_ref, sh_ref, w_ref, qg_ref, kvg_ref, cos_ref, sin_ref,
                     wuq_ref, wuk_ref, wuv_ref, q_ref, k_ref, v_ref):
    h = _norm_mod(x_ref[...], g_ref[...], sc_ref[...], sh_ref[...]).astype(BF16)
    y = _dot(h, w_ref[...])
    r0, r1, r2 = MLA_Q_RANK, MLA_Q_RANK + MLA_KV_RANK, MLA_Q_RANK + MLA_KV_RANK + MLA_ROPE_DIM
    cos2, sin2 = cos_ref[...], sin_ref[...]
    cq = _rms(y[:, :r0], qg_ref[...]).astype(BF16)
    ckv = _rms(y[:, r0:r1], kvg_ref[...]).astype(BF16)
    kr = _rope(y[:, r1:r2], cos2, sin2).astype(BF16)
    for hd in range(q_ref.shape[0]):
        yq = _dot(cq, wuq_ref[hd])
        qr = _rope(yq[:, MLA_NOPE_DIM:], cos2, sin2)
        q_ref[hd] = jnp.concatenate([yq[:, :MLA_NOPE_DIM], qr], axis=-1).astype(BF16)
        k_ref[hd] = jnp.concatenate([_dot(ckv, wuk_ref[hd]).astype(BF16), kr], axis=-1)
        v_ref[hd] = _dot(ckv, wuv_ref[hd]).astype(BF16)


def _mla_attn_kernel(q_ref, k_ref, v_ref, o_ref, *, tq, tk):
    dqk = q_ref.shape[2]
    nh, _, dv = v_ref.shape
    assert tq == tk
    c = dqk ** -0.5 * LOG2E
    ones = jnp.ones((tk, dv), BF16)

    def tile(j, carry, diagonal):
        k0 = pl.multiple_of(j * tk, tk)
        out = []
        for h in range(nh):
            m_i, acc = carry[h]
            s = _dot_nt(q_ref[h], k_ref[h, pl.ds(k0, tk), :])
            if diagonal:
                s = jnp.where(_iota((tq, tk), 1) <= _iota((tq, tk), 0), s, NEG)
            m_new = jnp.maximum(m_i, jnp.max(s, axis=-1, keepdims=True))
            alpha = jnp.exp2((m_i - m_new) * c)
            e = jnp.exp2((s - m_new) * c).astype(BF16)
            v_aug = jnp.concatenate([v_ref[h, pl.ds(k0, tk), :], ones], axis=1)
            out.append((m_new, alpha * acc + _dot(e, v_aug)))
        return tuple(out)

    i = pl.program_id(2)
    init = tuple((jnp.full((tq, 1), NEG, F32), jnp.zeros((tq, 2 * dv), F32)) for _ in range(nh))
    carry = lax.fori_loop(0, i, lambda j, cr: tile(j, cr, False), init)
    carry = tile(i, carry, True)
    for h in range(nh):
        acc = carry[h][1]
        o_ref[:, h * dv:(h + 1) * dv] = (acc[:, :dv] / jnp.maximum(acc[:, dv:], 1e-30)).astype(o_ref.dtype)


def _mla_layer(x2, mod, norm_g, p, batch, seq):
    m, d = x2.shape
    hh, dn, dr, dv = MLA_HEADS, MLA_NOPE_DIM, MLA_ROPE_DIM, MLA_V_DIM
    rq, rkv = MLA_Q_RANK, MLA_KV_RANK
    tm = 512
    tps = seq // tm

    pos = jnp.arange(seq, dtype=F32)
    inv_freq = ROPE_THETA ** (-jnp.arange(0, dr, 2, dtype=F32) / dr)
    ang = pos[:, None] * inv_freq[None, :]
    cos, sin = jnp.cos(ang), jnp.sin(ang)
    cos2 = jnp.concatenate([cos, cos], axis=-1)
    sin2 = jnp.concatenate([-sin, sin], axis=-1)
    rope_spec = pl.BlockSpec((tm, dr), lambda i, *_: (i % tps, 0))

    n_down = rq + rkv + dr
    n_pad = -n_down % 128
    w_down = jnp.pad(jnp.concatenate([p["w_dq"], p["w_dkv"]], axis=1), ((0, 0), (0, n_pad))).astype(BF16)
    dqk = dn + dr
    w_uq = jnp.transpose(p["w_uq"].reshape(rq, hh, dqk), (1, 0, 2)).astype(BF16)
    w_uk = jnp.transpose(p["w_uk"].reshape(rkv, hh, dn), (1, 0, 2)).astype(BF16)
    w_uv = jnp.transpose(p["w_uv"].reshape(rkv, hh, dv), (1, 0, 2)).astype(BF16)
    whole = lambda a: pl.BlockSpec(a.shape, lambda i: (0,) * a.ndim)
    head_out = lambda width: pl.BlockSpec((None, hh, tm, width), lambda i: (i // tps, 0, i % tps, 0))
    qf, kf, vf = pl.pallas_call(
        _mla_proj_kernel,
        out_shape=(jax.ShapeDtypeStruct((batch, hh, seq, dqk), BF16),
                   jax.ShapeDtypeStruct((batch, hh, seq, dqk), BF16),
                   jax.ShapeDtypeStruct((batch, hh, seq, dv), BF16)),
        grid=(m // tm,),
        in_specs=[pl.BlockSpec((tm, d), lambda i: (i, 0)),
                  _row_spec(d, False),
                  _mod_spec(1, tps, d, False), _mod_spec(0, tps, d, False),
                  whole(w_down), _row_spec(rq, False), _row_spec(rkv, False), rope_spec, rope_spec,
                  whole(w_uq), whole(w_uk), whole(w_uv)],
        out_specs=(head_out(dqk), head_out(dqk), head_out(dv)),
        compiler_params=_params("parallel"),
        name="mla_proj",
    )(x2, norm_g.reshape(1, d), mod, mod, w_down, p["q_norm_g"].reshape(1, rq),
      p["kv_norm_g"].reshape(1, rkv), cos2, sin2, w_uq, w_uk, w_uv)

    tq = tk = 512
    nh = 4
    o = pl.pallas_call(
        functools.partial(_mla_attn_kernel, tq=tq, tk=tk),
        out_shape=jax.ShapeDtypeStruct((batch, seq, hh * dv), BF16),
        grid=(batch, hh // nh, seq // tq),
        in_specs=[pl.BlockSpec((None, nh, tq, dqk), lambda b, h, i: (b, h, i, 0)),
                  pl.BlockSpec((None, nh, seq, dqk), lambda b, h, i: (b, h, 0, 0)),
                  pl.BlockSpec((None, nh, seq, dv), lambda b, h, i: (b, h, 0, 0))],
        out_specs=pl.BlockSpec((None, tq, nh * dv), lambda b, h, i: (b, i, h)),
        compiler_params=_params("parallel", "parallel", "arbitrary"),
        name="mla_attention",
    )(qf, kf, vf)
    return _mm_res(o.reshape(m, hh * dv), p["w_o"].astype(BF16), x2, mod, seq)


def kernel(x, c, ada_w, ada_b, norm1_g, norm2_g, mlp_w1, mlp_w2, final_g, conv_w_in, conv_b_in, conv_w_dw, conv_b_dw, conv_ln_g, conv_ln_b, conv_w_out, conv_b_out, nsa_w_q, nsa_w_kv_cmp, nsa_w_kv_slc, nsa_w_kv_win, nsa_cmp_pe, nsa_cmp_w1, nsa_cmp_w2, nsa_w_gate, nsa_b_gate, nsa_w_o, pool_w, pool_b, pool_scale, mla_w_dq, mla_q_norm_g, mla_w_uq, mla_w_dkv, mla_kv_norm_g, mla_w_uk, mla_w_uv, mla_w_o):
    batch, seq, d = x.shape
    depth = ada_w.shape[0]
    n_mixers = 4
    mods = _ada_mod(c, ada_w, ada_b)
    x2 = x.reshape(batch * seq, d)
    w1_all, w2_all = mlp_w1.astype(BF16), mlp_w2.astype(BF16)
    for i in range(depth):
        kind, u = i % n_mixers, i // n_mixers
        mod = mods[i].reshape(batch * 6, 1, d)
        if kind == 0:
            p = dict(w_in=conv_w_in[u], b_in=conv_b_in[u], w_dw=conv_w_dw[u], b_dw=conv_b_dw[u],
                     ln_g=conv_ln_g[u], ln_b=conv_ln_b[u], w_out=conv_w_out[u], b_out=conv_b_out[u])
            x2 = _conv_layer(x2, mod, norm1_g[i], p, seq)
        elif kind == 1:
            p = dict(w_q=nsa_w_q[u], w_kv_cmp=nsa_w_kv_cmp[u], w_kv_slc=nsa_w_kv_slc[u],
                     w_kv_win=nsa_w_kv_win[u], cmp_pe=nsa_cmp_pe[u], cmp_w1=nsa_cmp_w1[u],
                     cmp_w2=nsa_cmp_w2[u], w_gate=nsa_w_gate[u], b_gate=nsa_b_gate[u], w_o=nsa_w_o[u])
            x2 = _nsa_layer(x2, mod, norm1_g[i], p, batch, seq)
        elif kind == 2:
            p = dict(w=pool_w[u], b=pool_b[u], scale=pool_scale[u])
            x2 = _pool_layer(x2, mod, norm1_g[i], p, seq)
        else:
            p = dict(w_dq=mla_w_dq[u], q_norm_g=mla_q_norm_g[u], w_uq=mla_w_uq[u], w_dkv=mla_w_dkv[u],
                     kv_norm_g=mla_kv_norm_g[u], w_uk=mla_w_uk[u], w_uv=mla_w_uv[u], w_o=mla_w_o[u])
            x2 = _mla_layer(x2, mod, norm1_g[i], p, batch, seq)
        x2 = _mlp(x2, mod, norm2_g[i], w1_all, w2_all, i, seq, final_g=final_g if i == depth - 1 else None)
    return x2.reshape(batch, seq, d)
```

```python
import functools

import jax
import jax.numpy as jnp
from jax import lax
from jax.experimental import pallas as pl
from jax.experimental.pallas import tpu as pltpu

F32 = jnp.float32
BF16 = jnp.bfloat16

EPS = 1e-6
NEG = -1e30
FORCE = 1e30
LOG2E = 1.4426950408889634

CONV_WIDTH = 31
CONV_HALO = 32
NSA_HEADS = 16
NSA_HEAD_DIM = 128
NSA_KV_GROUPS = 4
NSA_HPG = NSA_HEADS // NSA_KV_GROUPS
NSA_GROUPS_PER_STEP = 4
CMP_BLOCK = 32
CMP_STRIDE = 16
SEL_BLOCK = 64
SEL_TOPK = 16
WINDOW = 512
POOL_WINDOWS = (2, 4, 8, 16)
POOL_HALO = 16
MLA_HEADS = 16
MLA_NOPE_DIM = 128
MLA_ROPE_DIM = 64
MLA_V_DIM = 128
MLA_Q_RANK = 512
MLA_KV_RANK = 256
ROPE_THETA = 10000.0

VMEM_LIMIT_BYTES = 56 * 1024 * 1024
LANES = 128


def _params(*semantics):
    return pltpu.CompilerParams(dimension_semantics=semantics, vmem_limit_bytes=VMEM_LIMIT_BYTES)


def _dot(a, b):
    return jnp.dot(a, b, preferred_element_type=F32)


def _dot_nt(a, b):
    return lax.dot_general(a, b, (((1,), (1,)), ((), ())), preferred_element_type=F32)


def _rms(x, g):
    return x * lax.rsqrt(jnp.mean(x * x, axis=-1, keepdims=True) + EPS) * g


def _norm_mod(x, g, sc, sh):
    return _rms(x, g) * (1.0 + sc) + sh


def _sigmoid(x):
    return 1.0 / (1.0 + jnp.exp(-x))


def _iota(shape, dim):
    return lax.broadcasted_iota(jnp.int32, shape, dim)


def _mod_spec(k, tiles_per_seq, width, tiled):
    if tiled:
        return pl.BlockSpec((None, 1, width), lambda i, j: ((i // tiles_per_seq) * 6 + k, 0, j))
    return pl.BlockSpec((None, 1, width), lambda i, *_: ((i // tiles_per_seq) * 6 + k, 0, 0))


def _row_spec(width, tiled):
    if tiled:
        return pl.BlockSpec((1, width), lambda i, j: (0, j))
    return pl.BlockSpec((1, width), lambda i, *_: (0, 0))


def _ada_kernel(c_ref, w_ref, b_ref, o_ref):
    c = c_ref[...]
    cs = c * _sigmoid(c)
    o_ref[...] = _dot(cs.astype(BF16), w_ref[...].astype(BF16)) + b_ref[...]


def _ada_mod(c, ada_w, ada_b):
    depth, d, n = ada_w.shape
    b = c.shape[0]
    bp = 8
    tn = 1024
    cp = jnp.pad(c, ((0, bp - b), (0, 0)))
    out = pl.pallas_call(
        _ada_kernel,
        out_shape=jax.ShapeDtypeStruct((depth, bp, n), F32),
        grid=(depth, n // tn),
        in_specs=[pl.BlockSpec((bp, d), lambda l, j: (0, 0)),
                  pl.BlockSpec((None, d, tn), lambda l, j: (l, 0, j)),
                  pl.BlockSpec((None, 1, tn), lambda l, j: (l, 0, j))],
        out_specs=pl.BlockSpec((None, bp, tn), lambda l, j: (l, 0, j)),
        compiler_params=_params("parallel", "parallel"),
        name="ada_mod",
    )(cp, ada_w, ada_b.reshape(depth, 1, n))
    return out[:, :b]


def _with_next_tile_norm(x0_ref, xn_ref, g_ref, sc_ref, sh_ref, scn_ref, shn_ref, h_even, h_odd, body):
    i, j = pl.program_id(0), pl.program_id(1)
    rows = xn_ref.shape[0]

    @pl.when((i == 0) & (j == 0))
    def _():
        h_even[...] = _norm_mod(x0_ref[...], g_ref[...], sc_ref[...], sh_ref[...]).astype(BF16)

    def step(h_cur, h_next):
        body(h_cur)
        r0 = pl.multiple_of(j * rows, rows)
        h_next[pl.ds(r0, rows), :] = _norm_mod(
            xn_ref[...], g_ref[...], scn_ref[...], shn_ref[...]).astype(BF16)

    @pl.when(i % 2 == 0)
    def _():
        step(h_even, h_odd)

    @pl.when(i % 2 == 1)
    def _():
        step(h_odd, h_even)


def _next_tile_specs(m, tm, steps, d, tiles_per_seq, k_scale, k_shift):
    n_tiles = m // tm
    rows = tm // steps
    assert rows * steps == tm and rows % 16 == 0
    nxt = lambda i: jnp.minimum(i + 1, n_tiles - 1)
    mod_next = lambda k: pl.BlockSpec((None, 1, d), lambda i, j: ((nxt(i) // tiles_per_seq) * 6 + k, 0, 0))
    return (pl.BlockSpec((rows, d), lambda i, j: (nxt(i) * steps + j, 0)), mod_next(k_scale), mod_next(k_shift))


def _mlp_kernel(x_ref, xn_ref, g_ref, sc_ref, sh_ref, scn_ref, shn_ref, gate_ref, w1_ref, w2_ref, *rest, final):
    if final:
        fg_ref, o_ref, h_even, h_odd, acc_ref = rest
    else:
        o_ref, h_even, h_odd, acc_ref = rest
    f = pl.program_id(1)

    @pl.when(f == 0)
    def _():
        acc_ref[...] = jnp.zeros_like(acc_ref)

    def body(h_cur):
        a = _dot(h_cur[...], w1_ref[...])
        a = jnp.square(jnp.maximum(a, 0.0)).astype(BF16)
        acc_ref[...] += _dot(a, w2_ref[...])

    _with_next_tile_norm(x_ref, xn_ref, g_ref, sc_ref, sh_ref, scn_ref, shn_ref, h_even, h_odd, body)

    @pl.when(f == pl.num_programs(1) - 1)
    def _():
        out = x_ref[...] + gate_ref[...] * acc_ref[...]
        if final:
            out = _rms(out, fg_ref[...])
        o_ref[...] = out


def _mlp(x2, mod, norm_g, w1, w2, layer, seq, final_g=None):
    m, d = x2.shape
    dff = w1.shape[2]
    tm, tf = 512, 1024
    tps = seq // tm
    final = final_g is not None
    nt, nf = m // tm, dff // tf
    xn_spec, scn_spec, shn_spec = _next_tile_specs(m, tm, nf, d, tps, 4, 3)
    in_specs = [pl.BlockSpec((tm, d), lambda i, f: (i, 0)),
                xn_spec,
                _row_spec(d, False),
                _mod_spec(4, tps, d, False), _mod_spec(3, tps, d, False), scn_spec, shn_spec,
                _mod_spec(5, tps, d, False),
                pl.BlockSpec((None, d, tf), lambda i, f: (layer, 0, f)),
                pl.BlockSpec((None, tf, d), lambda i, f: (layer, f, 0))]
    args = [x2, x2, norm_g.reshape(1, d), mod, mod, mod, mod, mod, w1, w2]
    if final:
        in_specs.append(_row_spec(d, False))
        args.append(final_g.reshape(1, d))
    return pl.pallas_call(
        functools.partial(_mlp_kernel, final=final),
        out_shape=jax.ShapeDtypeStruct((m, d), F32),
        grid=(nt, nf),
        in_specs=in_specs,
        out_specs=pl.BlockSpec((tm, d), lambda i, f: (i, 0)),
        scratch_shapes=[pltpu.VMEM((tm, d), BF16), pltpu.VMEM((tm, d), BF16), pltpu.VMEM((tm, d), F32)],
        compiler_params=_params("arbitrary", "arbitrary"),
        name="mlp",
    )(*args)


def _nm_mm_kernel(x0_ref, xn_ref, g_ref, sc_ref, sh_ref, scn_ref, shn_ref, w_ref, b_ref, o_ref, h_even, h_odd,
                  *, sig_from, tn):
    def body(h_cur):
        y = _dot(h_cur[...], w_ref[...]) + b_ref[...]
        if sig_from is not None:
            col = pl.program_id(1) * tn + _iota(y.shape, 1)
            y = jnp.where(col >= sig_from, _sigmoid(y), y)
        o_ref[...] = y.astype(o_ref.dtype)

    _with_next_tile_norm(x0_ref, xn_ref, g_ref, sc_ref, sh_ref, scn_ref, shn_ref, h_even, h_odd, body)


def _first_tile_specs(tm, d):
    first_mod = lambda k: pl.BlockSpec((None, 1, d), lambda i, j: (k, 0, 0))
    return pl.BlockSpec((tm, d), lambda i, j: (0, 0)), first_mod(1), first_mod(0)


def _nm_mm(x2, mod, norm_g, w, bias, seq, out_dtype, tn, sig_from=None):
    m, d = x2.shape
    n = w.shape[1]
    tm = 512
    tps = seq // tm
    x0_spec, sc0_spec, sh0_spec = _first_tile_specs(tm, d)
    xn_spec, scn_spec, shn_spec = _next_tile_specs(m, tm, n // tn, d, tps, 1, 0)
    return pl.pallas_call(
        functools.partial(_nm_mm_kernel, sig_from=sig_from, tn=tn),
        out_shape=jax.ShapeDtypeStruct((m, n), out_dtype),
        grid=(m // tm, n // tn),
        in_specs=[x0_spec, xn_spec, _row_spec(d, False), sc0_spec, sh0_spec, scn_spec, shn_spec,
                  pl.BlockSpec((d, tn), lambda i, j: (0, j)),
                  _row_spec(tn, True)],
        out_specs=pl.BlockSpec((tm, tn), lambda i, j: (i, j)),
        scratch_shapes=[pltpu.VMEM((tm, d), BF16), pltpu.VMEM((tm, d), BF16)],
        compiler_params=_params("arbitrary", "arbitrary"),
        name="norm_mod_matmul",
    )(x2, x2, norm_g.reshape(1, d), mod, mod, mod, mod, w, bias.reshape(1, n))


def _mm_res_kernel(a_ref, w_ref, x_ref, gate_ref, o_ref):
    o_ref[...] = x_ref[...] + gate_ref[...] * _dot(a_ref[...], w_ref[...])


def _mm_res(a, w, x2, mod, seq):
    m, k = a.shape
    n = w.shape[1]
    tm = 512
    tps = seq // tm
    return pl.pallas_call(
        _mm_res_kernel,
        out_shape=jax.ShapeDtypeStruct((m, n), F32),
        grid=(m // tm,),
        in_specs=[pl.BlockSpec((tm, k), lambda i: (i, 0)),
                  pl.BlockSpec((k, n), lambda i: (0, 0)),
                  pl.BlockSpec((tm, n), lambda i: (i, 0)),
                  _mod_spec(2, tps, n, False)],
        out_specs=pl.BlockSpec((tm, n), lambda i: (i, 0)),
        compiler_params=_params("parallel"),
        name="matmul_residual",
    )(a, w, x2, mod)


def _glu_kernel(x0_ref, xn_ref, g_ref, sc_ref, sh_ref, scn_ref, shn_ref, wa_ref, wg_ref, ba_ref, bg_ref,
                o_ref, h_even, h_odd):
    def body(h_cur):
        h = h_cur[...]
        a = _dot(h, wa_ref[...]) + ba_ref[...]
        gt = _dot(h, wg_ref[...]) + bg_ref[...]
        o_ref[...] = a * _sigmoid(gt)

    _with_next_tile_norm(x0_ref, xn_ref, g_ref, sc_ref, sh_ref, scn_ref, shn_ref, h_even, h_odd, body)


def _conv_kernel(u_ref, uh_ref, wdw_ref, bdw_ref, lng_ref, lnb_ref, wo_ref, bo_ref, x_ref, gate_ref,
                 o_ref, ubuf, shifted, conv_ref, v_ref, *, tm, tiles_per_seq):
    i = pl.program_id(0)
    d = u_ref.shape[1]
    rc = 64
    rn = 16
    span = tm + CONV_HALO - 8

    @pl.when(pl.program_id(1) == 0)
    def _():
        first = (i % tiles_per_seq) == 0
        ubuf[0:CONV_HALO, :] = jnp.where(first, 0.0, uh_ref[...])
        ubuf[CONV_HALO:, :] = u_ref[...]

        def column(ct, carry):
            cs = pl.ds(pl.multiple_of(ct * LANES, LANES), LANES)
            for r in range(1, 8):
                shifted[r - 1] = ubuf[r:r + span, cs]
            taps = [jnp.broadcast_to(wdw_ref[k:k + 1, cs], (8, LANES)) for k in range(CONV_WIDTH)]
            bias = jnp.broadcast_to(bdw_ref[:, cs], (8, LANES))

            def chunk(c, inner):
                r0 = pl.multiple_of(c * rc, rc)
                for v in range(rc // 8):
                    acc = bias
                    for k in range(CONV_WIDTH):
                        off = CONV_HALO - (CONV_WIDTH - 1) + k
                        q, r = divmod(off, 8)
                        start = pl.multiple_of(r0 + 8 * (q + v), 8)
                        rows = ubuf[pl.ds(start, 8), cs] if r == 0 else shifted[r - 1, pl.ds(start, 8), :]
                        acc = acc + taps[k] * rows
                    conv_ref[pl.ds(pl.multiple_of(r0 + 8 * v, 8), 8), cs] = acc
                return inner

            lax.fori_loop(0, tm // rc, chunk, 0)
            return carry

        lax.fori_loop(0, d // LANES, column, 0)

        def norm_chunk(c, carry):
            r0 = pl.multiple_of(c * rn, rn)
            acc = conv_ref[pl.ds(r0, rn), :]
            mu = jnp.mean(acc, axis=-1, keepdims=True)
            cen = acc - mu
            var = jnp.mean(cen * cen, axis=-1, keepdims=True)
            y = cen * lax.rsqrt(var + EPS) * lng_ref[...] + lnb_ref[...]
            v_ref[pl.ds(r0, rn), :] = (y * _sigmoid(y)).astype(BF16)
            return carry

        lax.fori_loop(0, tm // rn, norm_chunk, 0, unroll=4)

    y = _dot(v_ref[...], wo_ref[...]) + bo_ref[...]
    o_ref[...] = x_ref[...] + gate_ref[...] * y


def _conv_layer(x2, mod, norm_g, p, seq):
    m, d = x2.shape
    tm, tn = 512, 1024
    tps = seq // tm
    w_in = p["w_in"].astype(BF16)
    x0_spec, sc0_spec, sh0_spec = _first_tile_specs(tm, d)
    xn_spec, scn_spec, shn_spec = _next_tile_specs(m, tm, d // tn, d, tps, 1, 0)
    u = pl.pallas_call(
        _glu_kernel,
        out_shape=jax.ShapeDtypeStruct((m, d), F32),
        grid=(m // tm, d // tn),
        in_specs=[x0_spec, xn_spec, _row_spec(d, False), sc0_spec, sh0_spec, scn_spec, shn_spec,
                  pl.BlockSpec((d, tn), lambda i, j: (0, j)),
                  pl.BlockSpec((d, tn), lambda i, j: (0, j + d // tn)),
                  pl.BlockSpec((1, tn), lambda i, j: (0, j)),
                  pl.BlockSpec((1, tn), lambda i, j: (0, j + d // tn))],
        out_specs=pl.BlockSpec((tm, tn), lambda i, j: (i, j)),
        scratch_shapes=[pltpu.VMEM((tm, d), BF16), pltpu.VMEM((tm, d), BF16)],
        compiler_params=_params("arbitrary", "arbitrary"),
        name="conv_glu",
    )(x2, x2, norm_g.reshape(1, d), mod, mod, mod, mod, w_in, w_in,
      p["b_in"].reshape(1, 2 * d), p["b_in"].reshape(1, 2 * d))

    tm = 512
    tps = seq // tm
    hb = tm // CONV_HALO
    return pl.pallas_call(
        functools.partial(_conv_kernel, tm=tm, tiles_per_seq=tps),
        out_shape=jax.ShapeDtypeStruct((m, d), F32),
        grid=(m // tm, d // tn),
        in_specs=[pl.BlockSpec((tm, d), lambda i, j: (i, 0)),
                  pl.BlockSpec((CONV_HALO, d), lambda i, j: (jnp.maximum(i * hb - 1, 0), 0)),
                  pl.BlockSpec((CONV_WIDTH, d), lambda i, j: (0, 0)),
                  _row_spec(d, False), _row_spec(d, False), _row_spec(d, False),
                  pl.BlockSpec((d, tn), lambda i, j: (0, j)),
                  _row_spec(tn, True),
                  pl.BlockSpec((tm, tn), lambda i, j: (i, j)),
                  _mod_spec(2, tps, tn, True)],
        out_specs=pl.BlockSpec((tm, tn), lambda i, j: (i, j)),
        scratch_shapes=[pltpu.VMEM((tm + CONV_HALO, d), F32),
                        pltpu.VMEM((7, tm + CONV_HALO - 8, LANES), F32),
                        pltpu.VMEM((tm, d), F32),
                        pltpu.VMEM((tm, d), BF16)],
        compiler_params=_params("parallel", "arbitrary"),
        name="conv_ln_out",
    )(u, u, p["w_dw"], p["b_dw"].reshape(1, d), p["ln_g"].reshape(1, d), p["ln_b"].reshape(1, d),
      p["w_out"].astype(BF16), p["b_out"].reshape(1, d), x2, mod)


def _pool_kernel(x_ref, xh_ref, g_ref, sc_ref, sh_ref, gate_ref, w_ref, b_ref, scale_ref, o_ref, hbuf,
                 *, tm, tiles_per_seq):
    i = pl.program_id(0)
    d = x_ref.shape[1]
    gd = d // len(POOL_WINDOWS)
    first = (i % tiles_per_seq) == 0
    g, sc, sh = g_ref[...], sc_ref[...], sh_ref[...]
    hbuf[0:POOL_HALO, :] = jnp.where(first, 0.0, _norm_mod(xh_ref[...], g, sc, sh))
    hbuf[POOL_HALO:, :] = _norm_mod(x_ref[...], g, sc, sh)
    tpos = (i % tiles_per_seq) * tm + _iota((tm, 1), 0)
    for gi, win in enumerate(POOL_WINDOWS):
        lo, hi = gi * gd, (gi + 1) * gd
        h = hbuf[POOL_HALO:, lo:hi]
        wsum = h
        for k in range(1, win):
            wsum = wsum + hbuf[POOL_HALO - k:POOL_HALO - k + tm, lo:hi]
        cnt = jnp.minimum(tpos + 1, win).astype(F32)
        dlt = wsum / cnt - h
        y = (_dot(dlt.astype(BF16), w_ref[gi]) + b_ref[:, lo:hi]) * scale_ref[:, lo:hi]
        o_ref[:, lo:hi] = x_ref[:, lo:hi] + gate_ref[:, lo:hi] * y


def _pool_layer(x2, mod, norm_g, p, seq):
    m, d = x2.shape
    tm = 256
    tps = seq // tm
    hb = tm // POOL_HALO
    ng, gd = p["w"].shape[0], p["w"].shape[1]
    return pl.pallas_call(
        functools.partial(_pool_kernel, tm=tm, tiles_per_seq=tps),
        out_shape=jax.ShapeDtypeStruct((m, d), F32),
        grid=(m // tm,),
        in_specs=[pl.BlockSpec((tm, d), lambda i: (i, 0)),
                  pl.BlockSpec((POOL_HALO, d), lambda i: (jnp.maximum(i * hb - 1, 0), 0)),
                  _row_spec(d, False),
                  _mod_spec(1, tps, d, False), _mod_spec(0, tps, d, False), _mod_spec(2, tps, d, False),
                  pl.BlockSpec((ng, gd, gd), lambda i: (0, 0, 0)),
                  _row_spec(d, False), _row_spec(d, False)],
        out_specs=pl.BlockSpec((tm, d), lambda i: (i, 0)),
        scratch_shapes=[pltpu.VMEM((tm + POOL_HALO, d), F32)],
        compiler_params=_params("parallel"),
        name="pool_mixer",
    )(x2, x2, norm_g.reshape(1, d), mod, mod, mod, p["w"].astype(BF16),
      p["b"].reshape(1, d), p["scale"].reshape(1, d))


def _cmp_kernel(x_ref, pe_ref, w1_ref, w2_ref, o_ref, *, n_sub):
    dh = x_ref.shape[1]
    acc_a = jnp.zeros((n_sub, dh), F32)
    acc_b = jnp.zeros((n_sub, dh), F32)
    for p in range(CMP_STRIDE):
        xp = x_ref[pl.ds(p, n_sub, stride=CMP_STRIDE), :]
        acc_a = acc_a + _dot((xp + pe_ref[p:p + 1, :]).astype(BF16), w1_ref[p])
        q = CMP_STRIDE + p
        acc_b = acc_b + _dot((xp + pe_ref[q:q + 1, :]).astype(BF16), w1_ref[q])
    hid = acc_a + pltpu.roll(acc_b, n_sub - 1, 0)
    hid = hid * _sigmoid(hid)
    o_ref[...] = _dot(hid.astype(BF16), w2_ref[...])


def _nsa_compress(kvc3, pe, w1, w2):
    b, s, _ = kvc3.shape
    g, dh = NSA_KV_GROUPS, NSA_HEAD_DIM
    n_sub = s // CMP_STRIDE
    return pl.pallas_call(
        functools.partial(_cmp_kernel, n_sub=n_sub),
        out_shape=jax.ShapeDtypeStruct((2, b, g, n_sub, dh), F32),
        grid=(2, b, g),
        in_specs=[pl.BlockSpec((None, s, dh), lambda kv, bi, gi: (bi, 0, kv * g + gi)),
                  pl.BlockSpec((None, CMP_BLOCK, dh), lambda kv, bi, gi: (kv, 0, 0)),
                  pl.BlockSpec((None, CMP_BLOCK, dh, dh), lambda kv, bi, gi: (kv, 0, 0, 0)),
                  pl.BlockSpec((None, dh, dh), lambda kv, bi, gi: (kv, 0, 0))],
        out_specs=pl.BlockSpec((None, None, None, n_sub, dh), lambda kv, bi, gi: (kv, bi, gi, 0, 0)),
        compiler_params=_params("parallel", "parallel", "parallel"),
        name="nsa_compress",
    )(kvc3, pe, w1.astype(BF16), w2.astype(BF16))


def _sel_block_of(pos):
    assert SEL_BLOCK & (SEL_BLOCK - 1) == 0
    return jnp.right_shift(pos, SEL_BLOCK.bit_length() - 1)


def _split3(x):
    hi = x.astype(BF16)
    r = x - hi.astype(F32)
    mid = r.astype(BF16)
    lo = (r - mid.astype(F32)).astype(BF16)
    return hi, mid, lo


def _nsa_kernel(q_ref, ks_ref, vs_ref, kw_ref, vw_ref, kc_ref, vc_ref, gt_ref, at_ref, et_ref, wb_ref, o_ref,
                *, tq, tk):
    hpg, dh = NSA_HPG, NSA_HEAD_DIM
    assert tq == dh
    ng, n_cmp, _ = kc_ref.shape
    n_sel = at_ref.shape[0]
    rows = hpg * tq
    q0 = pl.program_id(2) * tq
    c = dh ** -0.5 * LOG2E
    tpos = q0 + _iota((tq, 1), 0)
    gcols = lambda g: slice(g * dh, (g + 1) * dh)

    def select(g):
        q = q_ref[:, g * hpg * dh:(g + 1) * hpg * dh]
        q4 = jnp.concatenate([q[:, h * dh:(h + 1) * dh] for h in range(hpg)], axis=0)
        sc = _dot_nt(q4, kc_ref[g].astype(BF16)).reshape(hpg, tq, n_cmp)
        cvalid = (_iota((tq, n_cmp), 1) * CMP_STRIDE + (CMP_BLOCK - 1)) <= tpos
        sc = sc + jnp.where(cvalid, 0.0, NEG)[None]
        e = jnp.exp2((sc - jnp.max(sc, axis=-1, keepdims=True)) * c)
        l = jnp.sum(e, axis=-1, keepdims=True)
        row_ok = jnp.where(tpos >= CMP_BLOCK - 1, 1.0, 0.0)
        p_cmp = e * (row_ok / jnp.maximum(l, 1e-30))
        o_cmp = _dot(p_cmp.reshape(rows, n_cmp).astype(BF16), vc_ref[g].astype(BF16))
        pg = p_cmp[0]
        for h in range(1, hpg):
            pg = pg + p_cmp[h]

        at = at_ref[...]
        imp_t = sum(_dot_nt(at, piece) for piece in _split3(pg))
        blk = _iota((n_sel, tq), 0)
        cur = _sel_block_of(q0 + _iota((n_sel, tq), 1))
        forced = (blk == 0) | (blk == cur) | (blk == cur - 1)
        val = jnp.where(forced, FORCE, jnp.where(blk > cur, NEG, imp_t))
        ranks = []
        for r0 in range(0, n_sel, 8):
            mine = val[r0:r0 + 8, :]
            rank = jnp.zeros((8, tq), F32)
            for mm in range(n_sel):
                other = val[mm:mm + 1, :]
                if mm < r0:
                    rank = rank + jnp.where(other >= mine, 1.0, 0.0)
                elif mm >= r0 + 8:
                    rank = rank + jnp.where(other > mine, 1.0, 0.0)
                else:
                    wins_tie = jnp.where(_iota((8, tq), 0) + r0 > mm, 1.0, 0.0)
                    rank = rank + jnp.where(other > mine, 1.0, jnp.where(other == mine, wins_tie, 0.0))
            ranks.append(rank)
        rank = jnp.concatenate(ranks, axis=0)
        keep = jnp.where(rank < SEL_TOPK, jnp.where(blk <= cur, 0.0, NEG), NEG)
        keep = jnp.concatenate([keep, jnp.zeros((dh - n_sel, tq), F32)], axis=0).T.astype(BF16)
        q_sel = jnp.concatenate([q4, jnp.concatenate([keep] * hpg, axis=0)], axis=1)
        return o_cmp, q4, q_sel

    o_cmp, q4, q_sel = zip(*[select(g) for g in range(ng)])

    def sel_tile(j, carry, last):
        k0 = pl.multiple_of(j * tk, tk)
        onehot = et_ref[pl.ds(k0, tk), :]
        out = []
        for g in range(ng):
            m_i, acc = carry[g]
            k_aug = jnp.concatenate([ks_ref[pl.ds(k0, tk), gcols(g)], onehot], axis=1)
            s = _dot_nt(q_sel[g], k_aug)
            if last:
                causal = (k0 + _iota((tq, tk), 1)) <= tpos
                s = jnp.where(causal[None], s.reshape(hpg, tq, tk), NEG).reshape(rows, tk)
            m_new = jnp.maximum(m_i, jnp.max(s, axis=-1, keepdims=True))
            alpha = jnp.exp2((m_i - m_new) * c)
            e = jnp.exp2((s - m_new) * c).astype(BF16)
            v_aug = jnp.concatenate([vs_ref[pl.ds(k0, tk), gcols(g)], jnp.ones((tk, dh), BF16)], axis=1)
            out.append((m_new, alpha * acc + _dot(e, v_aug)))
        return tuple(out)

    j_last = q0 // tk
    init = tuple((jnp.full((rows, 1), NEG, F32), jnp.zeros((rows, 2 * dh), F32)) for _ in range(ng))
    carry = lax.fori_loop(0, j_last, lambda j, cr: sel_tile(j, cr, False), init)
    carry = sel_tile(j_last, carry, True)

    band = WINDOW + tq
    k0w = pl.multiple_of(jnp.maximum(q0 - WINDOW, 0), tq)
    eye = jnp.where(_iota((rows, dh), 1) == (_iota((rows, dh), 0) & (tq - 1)), 1.0, 0.0).astype(BF16)
    for g in range(ng):
        acc = carry[g][1]
        o_sel = acc[:, :dh] / jnp.maximum(acc[:, dh:], 1e-30)
        q_win = jnp.concatenate([q4[g], eye], axis=1)
        k_aug = jnp.concatenate([kw_ref[pl.ds(k0w, band), gcols(g)], wb_ref[...]], axis=1)
        sw = _dot_nt(q_win, k_aug)
        e = jnp.exp2((sw - jnp.max(sw, axis=-1, keepdims=True)) * c).astype(BF16)
        v_aug = jnp.concatenate([vw_ref[pl.ds(k0w, band), gcols(g)], jnp.ones((band, dh), BF16)], axis=1)
        pv = _dot(e, v_aug)
        o_win = pv[:, :dh] / jnp.maximum(pv[:, dh:], 1e-30)

        gt = gt_ref[g]
        for h in range(hpg):
            g_cmp, g_sel, g_win = (gt[:, 3 * h + r:3 * h + r + 1] for r in range(3))
            hs = slice(h * tq, (h + 1) * tq)
            o_ref[:, (g * hpg + h) * dh:(g * hpg + h + 1) * dh] = (
                g_cmp * o_cmp[g][hs] + g_sel * o_sel[hs] + g_win * o_win[hs]).astype(o_ref.dtype)


def _nsa_attention(proj, kv_cmp, gates, seq):
    b = proj.shape[0]
    g, hpg, dh = NSA_KV_GROUPS, NSA_HPG, NSA_HEAD_DIM
    tq, tk = 128, 512
    n_cmp = kv_cmp.shape[3]
    n_sel = seq // SEL_BLOCK
    d_q = g * hpg * dh
    col_q, col_slc, col_win = 0, d_q // dh, (d_q + 2 * g * dh) // dh

    sub = jnp.arange(n_cmp)[None, :] // (SEL_BLOCK // CMP_STRIDE)
    nxt = (jnp.arange(n_cmp)[None, :] + 1) // (SEL_BLOCK // CMP_STRIDE)
    rows = jnp.arange(n_sel)[:, None]
    agg = ((sub == rows).astype(F32) + (nxt == rows).astype(F32)).astype(BF16)
    assert n_sel <= dh
    block_onehot = (jnp.arange(seq)[:, None] // SEL_BLOCK == jnp.arange(dh)[None, :]).astype(BF16)
    band = WINDOW + tq
    n_pat = WINDOW // tq + 1
    pat_q0 = jnp.arange(n_pat)[:, None, None] * tq
    key = jnp.maximum(pat_q0 - WINDOW, 0) + jnp.arange(band)[None, :, None]
    qry = pat_q0 + jnp.arange(tq)[None, None, :]
    win_bias = jnp.where((key <= qry) & (key > qry - WINDOW), 0.0, NEG).astype(BF16)

    ng = NSA_GROUPS_PER_STEP
    assert g % ng == 0 and col_slc % ng == 0 and col_win % ng == 0
    kv_spec = lambda col: pl.BlockSpec((None, seq, ng * dh), lambda bi, gi, i: (bi, 0, col // ng + gi))
    cmp_spec = lambda kv: pl.BlockSpec((None, None, ng, n_cmp, dh), lambda bi, gi, i: (kv, bi, gi, 0, 0))
    return pl.pallas_call(
        functools.partial(_nsa_kernel, tq=tq, tk=tk),
        out_shape=jax.ShapeDtypeStruct((b, seq, d_q), BF16),
        grid=(b, g // ng, seq // tq),
        in_specs=[pl.BlockSpec((None, tq, ng * hpg * dh), lambda bi, gi, i: (bi, i, col_q + gi)),
                  kv_spec(col_slc), kv_spec(col_slc + g), kv_spec(col_win), kv_spec(col_win + g),
                  cmp_spec(0), cmp_spec(1),
                  pl.BlockSpec((None, ng, tq, 3 * hpg), lambda bi, gi, i: (bi, gi, i, 0)),
                  pl.BlockSpec((n_sel, n_cmp), lambda bi, gi, i: (0, 0)),
                  pl.BlockSpec((seq, dh), lambda bi, gi, i: (0, 0)),
                  pl.BlockSpec((None, band, tq), lambda bi, gi, i: (jnp.minimum(i, n_pat - 1), 0, 0))],
        out_specs=pl.BlockSpec((None, tq, ng * hpg * dh), lambda bi, gi, i: (bi, i, gi)),
        compiler_params=_params("parallel", "parallel", "arbitrary"),
        name="nsa_attention",
    )(proj, proj, proj, proj, proj, kv_cmp, kv_cmp, gates, agg, block_onehot, win_bias)


def _nsa_layer(x2, mod, norm_g, p, batch, seq):
    m, d = x2.shape
    g, hpg, dh = NSA_KV_GROUPS, NSA_HPG, NSA_HEAD_DIM
    w_a = jnp.concatenate([p["w_q"], p["w_kv_slc"], p["w_kv_win"]], axis=1).astype(BF16)
    n_gate = p["w_gate"].shape[1]
    gate_pad = 128 - n_gate
    w_b = jnp.concatenate([p["w_kv_cmp"], jnp.pad(p["w_gate"], ((0, 0), (0, gate_pad)))], axis=1).astype(BF16)
    n_cmp_cols = p["w_kv_cmp"].shape[1]
    b_b = jnp.concatenate([jnp.zeros((n_cmp_cols,), F32), jnp.pad(p["b_gate"], (0, gate_pad))])
    proj = _nm_mm(x2, mod, norm_g, w_a, jnp.zeros((w_a.shape[1],), F32), seq, BF16, tn=1024)
    proj_b = _nm_mm(x2, mod, norm_g, w_b, b_b, seq, F32, tn=w_b.shape[1], sig_from=n_cmp_cols)
    proj_b3 = proj_b.reshape(batch, seq, proj_b.shape[1])
    kv_cmp = _nsa_compress(proj_b3, p["cmp_pe"], p["cmp_w1"], p["cmp_w2"])
    gates = proj_b3[:, :, n_cmp_cols:n_cmp_cols + n_gate].reshape(batch, seq, g, 3 * hpg)
    gates = jnp.transpose(gates, (0, 2, 1, 3))
    o = _nsa_attention(proj.reshape(batch, seq, proj.shape[1]), kv_cmp, gates, seq)
    return _mm_res(o.reshape(m, d), p["w_o"].astype(BF16), x2, mod, seq)


def _rope(x, cos2, sin2):
    half = x.shape[1] // 2
    swapped = jnp.concatenate([x[:, half:], x[:, :half]], axis=-1)
    return x * cos2 + swapped * sin2


def _mla_proj_kernel(x_ref, g_ref, sc_ref, sh_ref, w_ref, qg_ref, kvg_ref, cos_ref, sin_ref,
                     wuq_ref, wuk_ref, wuv_ref, q_ref, k_ref, v_ref):
    h = _norm_mod(x_ref[...], g_ref[...], sc_ref[...], sh_ref[...]).astype(BF16)
    y = _dot(h, w_ref[...])
    r0, r1, r2 = MLA_Q_RANK, MLA_Q_RANK + MLA_KV_RANK, MLA_Q_RANK + MLA_KV_RANK + MLA_ROPE_DIM
    cos2, sin2 = cos_ref[...], sin_ref[...]
    cq = _rms(y[:, :r0], qg_ref[...]).astype(BF16)
    ckv = _rms(y[:, r0:r1], kvg_ref[...]).astype(BF16)
    kr = _rope(y[:, r1:r2], cos2, sin2).astype(BF16)
    for hd in range(q_ref.shape[0]):
        yq = _dot(cq, wuq_ref[hd])
        qr = _rope(yq[:, MLA_NOPE_DIM:], cos2, sin2)
        q_ref[hd] = jnp.concatenate([yq[:, :MLA_NOPE_DIM], qr], axis=-1).astype(BF16)
        k_ref[hd] = jnp.concatenate([_dot(ckv, wuk_ref[hd]).astype(BF16), kr], axis=-1)
        v_ref[hd] = _dot(ckv, wuv_ref[hd]).astype(BF16)


def _mla_attn_kernel(q_ref, k_ref, v_ref, o_ref, *, tq, tk):
    dqk = q_ref.shape[2]
    nh, _, dv = v_ref.shape
    assert tq == tk
    c = dqk ** -0.5 * LOG2E
    ones = jnp.ones((tk, dv), BF16)

    def tile(j, carry, diagonal):
        k0 = pl.multiple_of(j * tk, tk)
        out = []
        for h in range(nh):
            m_i, acc = carry[h]
            s = _dot_nt(q_ref[h], k_ref[h, pl.ds(k0, tk), :])
            if diagonal:
                s = jnp.where(_iota((tq, tk), 1) <= _iota((tq, tk), 0), s, NEG)
            m_new = jnp.maximum(m_i, jnp.max(s, axis=-1, keepdims=True))
            alpha = jnp.exp2((m_i - m_new) * c)
            e = jnp.exp2((s - m_new) * c).astype(BF16)
            v_aug = jnp.concatenate([v_ref[h, pl.ds(k0, tk), :], ones], axis=1)
            out.append((m_new, alpha * acc + _dot(e, v_aug)))
        return tuple(out)

    i = pl.program_id(2)
    init = tuple((jnp.full((tq, 1), NEG, F32), jnp.zeros((tq, 2 * dv), F32)) for _ in range(nh))
    carry = lax.fori_loop(0, i, lambda j, cr: tile(j, cr, False), init)
    carry = tile(i, carry, True)
    for h in range(nh):
        acc = carry[h][1]
        o_ref[:, h * dv:(h + 1) * dv] = (acc[:, :dv] / jnp.maximum(acc[:, dv:], 1e-30)).astype(o_ref.dtype)


def _mla_layer(x2, mod, norm_g, p, batch, seq):
    m, d = x2.shape
    hh, dn, dr, dv = MLA_HEADS, MLA_NOPE_DIM, MLA_ROPE_DIM, MLA_V_DIM
    rq, rkv = MLA_Q_RANK, MLA_KV_RANK
    tm = 512
    tps = seq // tm

    pos = jnp.arange(seq, dtype=F32)
    inv_freq = ROPE_THETA ** (-jnp.arange(0, dr, 2, dtype=F32) / dr)
    ang = pos[:, None] * inv_freq[None, :]
    cos, sin = jnp.cos(ang), jnp.sin(ang)
    cos2 = jnp.concatenate([cos, cos], axis=-1)
    sin2 = jnp.concatenate([-sin, sin], axis=-1)
    rope_spec = pl.BlockSpec((tm, dr), lambda i, *_: (i % tps, 0))

    n_down = rq + rkv + dr
    n_pad = -n_down % 128
    w_down = jnp.pad(jnp.concatenate([p["w_dq"], p["w_dkv"]], axis=1), ((0, 0), (0, n_pad))).astype(BF16)
    dqk = dn + dr
    w_uq = jnp.transpose(p["w_uq"].reshape(rq, hh, dqk), (1, 0, 2)).astype(BF16)
    w_uk = jnp.transpose(p["w_uk"].reshape(rkv, hh, dn), (1, 0, 2)).astype(BF16)
    w_uv = jnp.transpose(p["w_uv"].reshape(rkv, hh, dv), (1, 0, 2)).astype(BF16)
    whole = lambda a: pl.BlockSpec(a.shape, lambda i: (0,) * a.ndim)
    head_out = lambda width: pl.BlockSpec((None, hh, tm, width), lambda i: (i // tps, 0, i % tps, 0))
    qf, kf, vf = pl.pallas_call(
        _mla_proj_kernel,
        out_shape=(jax.ShapeDtypeStruct((batch, hh, seq, dqk), BF16),
                   jax.ShapeDtypeStruct((batch, hh, seq, dqk), BF16),
                   jax.ShapeDtypeStruct((batch, hh, seq, dv), BF16)),
        grid=(m // tm,),
        in_specs=[pl.BlockSpec((tm, d), lambda i: (i, 0)),
                  _row_spec(d, False),
                  _mod_spec(1, tps, d, False), _mod_spec(0, tps, d, False),
                  whole(w_down), _row_spec(rq, False), _row_spec(rkv, False), rope_spec, rope_spec,
                  whole(w_uq), whole(w_uk), whole(w_uv)],
        out_specs=(head_out(dqk), head_out(dqk), head_out(dv)),
        compiler_params=_params("parallel"),
        name="mla_proj",
    )(x2, norm_g.reshape(1, d), mod, mod, w_down, p["q_norm_g"].reshape(1, rq),
      p["kv_norm_g"].reshape(1, rkv), cos2, sin2, w_uq, w_uk, w_uv)

    tq = tk = 512
    nh = 4
    o = pl.pallas_call(
        functools.partial(_mla_attn_kernel, tq=tq, tk=tk),
        out_shape=jax.ShapeDtypeStruct((batch, seq, hh * dv), BF16),
        grid=(batch, hh // nh, seq // tq),
        in_specs=[pl.BlockSpec((None, nh, tq, dqk), lambda b, h, i: (b, h, i, 0)),
                  pl.BlockSpec((None, nh, seq, dqk), lambda b, h, i: (b, h, 0, 0)),
                  pl.BlockSpec((None, nh, seq, dv), lambda b, h, i: (b, h, 0, 0))],
        out_specs=pl.BlockSpec((None, tq, nh * dv), lambda b, h, i: (b, i, h)),
        compiler_params=_params("parallel", "parallel", "arbitrary"),
        name="mla_attention",
    )(qf, kf, vf)
    return _mm_res(o.reshape(m, hh * dv), p["w_o"].astype(BF16), x2, mod, seq)


def kernel(x, c, ada_w, ada_b, norm1_g, norm2_g, mlp_w1, mlp_w2, final_g, conv_w_in, conv_b_in, conv_w_dw, conv_b_dw, conv_ln_g, conv_ln_b, conv_w_out, conv_b_out, nsa_w_q, nsa_w_kv_cmp, nsa_w_kv_slc, nsa_w_kv_win, nsa_cmp_pe, nsa_cmp_w1, nsa_cmp_w2, nsa_w_gate, nsa_b_gate, nsa_w_o, pool_w, pool_b, pool_scale, mla_w_dq, mla_q_norm_g, mla_w_uq, mla_w_dkv, mla_kv_norm_g, mla_w_uk, mla_w_uv, mla_w_o):
    batch, seq, d = x.shape
    depth = ada_w.shape[0]
    n_mixers = 4
    mods = _ada_mod(c, ada_w, ada_b)
    x2 = x.reshape(batch * seq, d)
    w1_all, w2_all = mlp_w1.astype(BF16), mlp_w2.astype(BF16)
    for i in range(depth):
        kind, u = i % n_mixers, i // n_mixers
        mod = mods[i].reshape(batch * 6, 1, d)
        if kind == 0:
            p = dict(w_in=conv_w_in[u], b_in=conv_b_in[u], w_dw=conv_w_dw[u], b_dw=conv_b_dw[u],
                     ln_g=conv_ln_g[u], ln_b=conv_ln_b[u], w_out=conv_w_out[u], b_out=conv_b_out[u])
            x2 = _conv_layer(x2, mod, norm1_g[i], p, seq)
        elif kind == 1:
            p = dict(w_q=nsa_w_q[u], w_kv_cmp=nsa_w_kv_cmp[u], w_kv_slc=nsa_w_kv_slc[u],
                     w_kv_win=nsa_w_kv_win[u], cmp_pe=nsa_cmp_pe[u], cmp_w1=nsa_cmp_w1[u],
                     cmp_w2=nsa_cmp_w2[u], w_gate=nsa_w_gate[u], b_gate=nsa_b_gate[u], w_o=nsa_w_o[u])
            x2 = _nsa_layer(x2, mod, norm1_g[i], p, batch, seq)
        elif kind == 2:
            p = dict(w=pool_w[u], b=pool_b[u], scale=pool_scale[u])
            x2 = _pool_layer(x2, mod, norm1_g[i], p, seq)
        else:
            p = dict(w_dq=mla_w_dq[u], q_norm_g=mla_q_norm_g[u], w_uq=mla_w_uq[u], w_dkv=mla_w_dkv[u],
                     kv_norm_g=mla_kv_norm_g[u], w_uk=mla_w_uk[u], w_uv=mla_w_uv[u], w_o=mla_w_o[u])
            x2 = _mla_layer(x2, mod, norm1_g[i], p, batch, seq)
        x2 = _mlp(x2, mod, norm2_g[i], w1_all, w2_all, i, seq, final_g=final_g if i == depth - 1 else None)
    return x2.reshape(batch, seq, d)
```

```python
import functools

import jax
import jax.numpy as jnp
from jax import lax
from jax.experimental import pallas as pl
from jax.experimental.pallas import tpu as pltpu

F32 = jnp.float32
BF16 = jnp.bfloat16

EPS = 1e-6
NEG = -1e30
FORCE = 1e30
LOG2E = 1.4426950408889634

CONV_WIDTH = 31
CONV_HALO = 32
NSA_HEADS = 16
NSA_HEAD_DIM = 128
NSA_KV_GROUPS = 4
NSA_HPG = NSA_HEADS // NSA_KV_GROUPS
NSA_GROUPS_PER_STEP = 4
CMP_BLOCK = 32
CMP_STRIDE = 16
SEL_BLOCK = 64
SEL_TOPK = 16
WINDOW = 512
POOL_WINDOWS = (2, 4, 8, 16)
POOL_HALO = 16
MLA_HEADS = 16
MLA_NOPE_DIM = 128
MLA_ROPE_DIM = 64
MLA_V_DIM = 128
MLA_Q_RANK = 512
MLA_KV_RANK = 256
ROPE_THETA = 10000.0

VMEM_LIMIT_BYTES = 56 * 1024 * 1024
LANES = 128


def _params(*semantics):
    return pltpu.CompilerParams(dimension_semantics=semantics, vmem_limit_bytes=VMEM_LIMIT_BYTES)


def _dot(a, b):
    return jnp.dot(a, b, preferred_element_type=F32)


def _dot_nt(a, b):
    return lax.dot_general(a, b, (((1,), (1,)), ((), ())), preferred_element_type=F32)


def _rms(x, g):
    return x * lax.rsqrt(jnp.mean(x * x, axis=-1, keepdims=True) + EPS) * g


def _norm_mod(x, g, sc, sh):
    return _rms(x, g) * (1.0 + sc) + sh


def _sigmoid(x):
    return 1.0 / (1.0 + jnp.exp(-x))


def _iota(shape, dim):
    return lax.broadcasted_iota(jnp.int32, shape, dim)


def _mod_spec(k, tiles_per_seq, width, tiled):
    if tiled:
        return pl.BlockSpec((None, 1, width), lambda i, j: ((i // tiles_per_seq) * 6 + k, 0, j))
    return pl.BlockSpec((None, 1, width), lambda i, *_: ((i // tiles_per_seq) * 6 + k, 0, 0))


def _row_spec(width, tiled):
    if tiled:
        return pl.BlockSpec((1, width), lambda i, j: (0, j))
    return pl.BlockSpec((1, width), lambda i, *_: (0, 0))


def _ada_kernel(c_ref, w_ref, b_ref, o_ref):
    c = c_ref[...]
    cs = c * _sigmoid(c)
    o_ref[...] = _dot(cs.astype(BF16), w_ref[...].astype(BF16)) + b_ref[...]


def _ada_mod(c, ada_w, ada_b):
    depth, d, n = ada_w.shape
    b = c.shape[0]
    bp = 8
    tn = 1024
    cp = jnp.pad(c, ((0, bp - b), (0, 0)))
    out = pl.pallas_call(
        _ada_kernel,
        out_shape=jax.ShapeDtypeStruct((depth, bp, n), F32),
        grid=(depth, n // tn),
        in_specs=[pl.BlockSpec((bp, d), lambda l, j: (0, 0)),
                  pl.BlockSpec((None, d, tn), lambda l, j: (l, 0, j)),
                  pl.BlockSpec((None, 1, tn), lambda l, j: (l, 0, j))],
        out_specs=pl.BlockSpec((None, bp, tn), lambda l, j: (l, 0, j)),
        compiler_params=_params("parallel", "parallel"),
        name="ada_mod",
    )(cp, ada_w, ada_b.reshape(depth, 1, n))
    return out[:, :b]


def _with_next_tile_norm(x0_ref, xn_ref, g_ref, sc_ref, sh_ref, scn_ref, shn_ref, h_even, h_odd, body):
    i, j = pl.program_id(0), pl.program_id(1)
    rows = xn_ref.shape[0]

    @pl.when((i == 0) & (j == 0))
    def _():
        h_even[...] = _norm_mod(x0_ref[...], g_ref[...], sc_ref[...], sh_ref[...]).astype(BF16)

    def step(h_cur, h_next):
        body(h_cur)
        r0 = pl.multiple_of(j * rows, rows)
        h_next[pl.ds(r0, rows), :] = _norm_mod(
            xn_ref[...], g_ref[...], scn_ref[...], shn_ref[...]).astype(BF16)

    @pl.when(i % 2 == 0)
    def _():
        step(h_even, h_odd)

    @pl.when(i % 2 == 1)
    def _():
        step(h_odd, h_even)


def _next_tile_specs(m, tm, steps, d, tiles_per_seq, k_scale, k_shift):
    n_tiles = m // tm
    rows = tm // steps
    assert rows * steps == tm and rows % 16 == 0
    nxt = lambda i: jnp.minimum(i + 1, n_tiles - 1)
    mod_next = lambda k: pl.BlockSpec((None, 1, d), lambda i, j: ((nxt(i) // tiles_per_seq) * 6 + k, 0, 0))
    return (pl.BlockSpec((rows, d), lambda i, j: (nxt(i) * steps + j, 0)), mod_next(k_scale), mod_next(k_shift))


def _mlp_kernel(x_ref, xn_ref, g_ref, sc_ref, sh_ref, scn_ref, shn_ref, gate_ref, w1_ref, w2_ref, *rest, final):
    if final:
        fg_ref, o_ref, h_even, h_odd, acc_ref = rest
    else:
        o_ref, h_even, h_odd, acc_ref = rest
    f = pl.program_id(1)

    @pl.when(f == 0)
    def _():
        acc_ref[...] = jnp.zeros_like(acc_ref)

    def body(h_cur):
        a = _dot(h_cur[...], w1_ref[...])
        a = jnp.square(jnp.maximum(a, 0.0)).astype(BF16)
        acc_ref[...] += _dot(a, w2_ref[...])

    _with_next_tile_norm(x_ref, xn_ref, g_ref, sc_ref, sh_ref, scn_ref, shn_ref, h_even, h_odd, body)

    @pl.when(f == pl.num_programs(1) - 1)
    def _():
        out = x_ref[...] + gate_ref[...] * acc_ref[...]
        if final:
            out = _rms(out, fg_ref[...])
        o_ref[...] = out


def _mlp(x2, mod, norm_g, w1, w2, layer, seq, final_g=None):
    m, d = x2.shape
    dff = w1.shape[2]
    tm, tf = 512, 1024
    tps = seq // tm
    final = final_g is not None
    nt, nf = m // tm, dff // tf
    xn_spec, scn_spec, shn_spec = _next_tile_specs(m, tm, nf, d, tps, 4, 3)
    in_specs = [pl.BlockSpec((tm, d), lambda i, f: (i, 0)),
                xn_spec,
                _row_spec(d, False),
                _mod_spec(4, tps, d, False), _mod_spec(3, tps, d, False), scn_spec, shn_spec,
                _mod_spec(5, tps, d, False),
                pl.BlockSpec((None, d, tf), lambda i, f: (layer, 0, f)),
                pl.BlockSpec((None, tf, d), lambda i, f: (layer, f, 0))]
    args = [x2, x2, norm_g.reshape(1, d), mod, mod, mod, mod, mod, w1, w2]
    if final:
        in_specs.append(_row_spec(d, False))
        args.append(final_g.reshape(1, d))
    return pl.pallas_call(
        functools.partial(_mlp_kernel, final=final),
        out_shape=jax.ShapeDtypeStruct((m, d), F32),
        grid=(nt, nf),
        in_specs=in_specs,
        out_specs=pl.BlockSpec((tm, d), lambda i, f: (i, 0)),
        scratch_shapes=[pltpu.VMEM((tm, d), BF16), pltpu.VMEM((tm, d), BF16), pltpu.VMEM((tm, d), F32)],
        compiler_params=_params("arbitrary", "arbitrary"),
        name="mlp",
    )(*args)


def _nm_mm_kernel(x0_ref, xn_ref, g_ref, sc_ref, sh_ref, scn_ref, shn_ref, w_ref, b_ref, o_ref, h_even, h_odd,
                  *, sig_from, tn):
    def body(h_cur):
        y = _dot(h_cur[...], w_ref[...]) + b_ref[...]
        if sig_from is not None:
            col = pl.program_id(1) * tn + _iota(y.shape, 1)
            y = jnp.where(col >= sig_from, _sigmoid(y), y)
        o_ref[...] = y.astype(o_ref.dtype)

    _with_next_tile_norm(x0_ref, xn_ref, g_ref, sc_ref, sh_ref, scn_ref, shn_ref, h_even, h_odd, body)


def _first_tile_specs(tm, d):
    first_mod = lambda k: pl.BlockSpec((None, 1, d), lambda i, j: (k, 0, 0))
    return pl.BlockSpec((tm, d), lambda i, j: (0, 0)), first_mod(1), first_mod(0)


def _nm_mm(x2, mod, norm_g, w, bias, seq, out_dtype, tn, sig_from=None):
    m, d = x2.shape
    n = w.shape[1]
    tm = 512
    tps = seq // tm
    x0_spec, sc0_spec, sh0_spec = _first_tile_specs(tm, d)
    xn_spec, scn_spec, shn_spec = _next_tile_specs(m, tm, n // tn, d, tps, 1, 0)
    return pl.pallas_call(
        functools.partial(_nm_mm_kernel, sig_from=sig_from, tn=tn),
        out_shape=jax.ShapeDtypeStruct((m, n), out_dtype),
        grid=(m // tm, n // tn),
        in_specs=[x0_spec, xn_spec, _row_spec(d, False), sc0_spec, sh0_spec, scn_spec, shn_spec,
                  pl.BlockSpec((d, tn), lambda i, j: (0, j)),
                  _row_spec(tn, True)],
        out_specs=pl.BlockSpec((tm, tn), lambda i, j: (i, j)),
        scratch_shapes=[pltpu.VMEM((tm, d), BF16), pltpu.VMEM((tm, d), BF16)],
        compiler_params=_params("arbitrary", "arbitrary"),
        name="norm_mod_matmul",
    )(x2, x2, norm_g.reshape(1, d), mod, mod, mod, mod, w, bias.reshape(1, n))


def _mm_res_kernel(a_ref, w_ref, x_ref, gate_ref, o_ref):
    o_ref[...] = x_ref[...] + gate_ref[...] * _dot(a_ref[...], w_ref[...])


def _mm_res(a, w, x2, mod, seq):
    m, k = a.shape
    n = w.shape[1]
    tm = 512
    tps = seq // tm
    return pl.pallas_call(
        _mm_res_kernel,
        out_shape=jax.ShapeDtypeStruct((m, n), F32),
        grid=(m // tm,),
        in_specs=[pl.BlockSpec((tm, k), lambda i: (i, 0)),
                  pl.BlockSpec((k, n), lambda i: (0, 0)),
                  pl.BlockSpec((tm, n), lambda i: (i, 0)),
                  _mod_spec(2, tps, n, False)],
        out_specs=pl.BlockSpec((tm, n), lambda i: (i, 0)),
        compiler_params=_params("parallel"),
        name="matmul_residual",
    )(a, w, x2, mod)


def _glu_kernel(x0_ref, xn_ref, g_ref, sc_ref, sh_ref, scn_ref, shn_ref, wa_ref, wg_ref, ba_ref, bg_ref,
                o_ref, h_even, h_odd):
    def body(h_cur):
        h = h_cur[...]
        a = _dot(h, wa_ref[...]) + ba_ref[...]
        gt = _dot(h, wg_ref[...]) + bg_ref[...]
        o_ref[...] = a * _sigmoid(gt)

    _with_next_tile_norm(x0_ref, xn_ref, g_ref, sc_ref, sh_ref, scn_ref, shn_ref, h_even, h_odd, body)


def _conv_kernel(u_ref, uh_ref, wdw_ref, bdw_ref, lng_ref, lnb_ref, wo_ref, bo_ref, x_ref, gate_ref,
                 o_ref, ubuf, shifted, conv_ref, v_ref, *, tm, tiles_per_seq):
    i = pl.program_id(0)
    d = u_ref.shape[1]
    rc = 64
    rn = 16
    span = tm + CONV_HALO - 8

    @pl.when(pl.program_id(1) == 0)
    def _():
        first = (i % tiles_per_seq) == 0
        ubuf[0:CONV_HALO, :] = jnp.where(first, 0.0, uh_ref[...])
        ubuf[CONV_HALO:, :] = u_ref[...]

        def column(ct, carry):
            cs = pl.ds(pl.multiple_of(ct * LANES, LANES), LANES)
            for r in range(1, 8):
                shifted[r - 1] = ubuf[r:r + span, cs]
            taps = [jnp.broadcast_to(wdw_ref[k:k + 1, cs], (8, LANES)) for k in range(CONV_WIDTH)]
            bias = jnp.broadcast_to(bdw_ref[:, cs], (8, LANES))

            def chunk(c, inner):
                r0 = pl.multiple_of(c * rc, rc)
                for v in range(rc // 8):
                    acc = bias
                    for k in range(CONV_WIDTH):
                        off = CONV_HALO - (CONV_WIDTH - 1) + k
                        q, r = divmod(off, 8)
                        start = pl.multiple_of(r0 + 8 * (q + v), 8)
                        rows = ubuf[pl.ds(start, 8), cs] if r == 0 else shifted[r - 1, pl.ds(start, 8), :]
                        acc = acc + taps[k] * rows
                    conv_ref[pl.ds(pl.multiple_of(r0 + 8 * v, 8), 8), cs] = acc
                return inner

            lax.fori_loop(0, tm // rc, chunk, 0)
            return carry

        lax.fori_loop(0, d // LANES, column, 0)

        def norm_chunk(c, carry):
            r0 = pl.multiple_of(c * rn, rn)
            acc = conv_ref[pl.ds(r0, rn), :]
            mu = jnp.mean(acc, axis=-1, keepdims=True)
            cen = acc - mu
            var = jnp.mean(cen * cen, axis=-1, keepdims=True)
            y = cen * lax.rsqrt(var + EPS) * lng_ref[...] + lnb_ref[...]
            v_ref[pl.ds(r0, rn), :] = (y * _sigmoid(y)).astype(BF16)
            return carry

        lax.fori_loop(0, tm // rn, norm_chunk, 0, unroll=4)

    y = _dot(v_ref[...], wo_ref[...]) + bo_ref[...]
    o_ref[...] = x_ref[...] + gate_ref[...] * y


def _conv_layer(x2, mod, norm_g, p, seq):
    m, d = x2.shape
    tm, tn = 512, 1024
    tps = seq // tm
    w_in = p["w_in"].astype(BF16)
    x0_spec, sc0_spec, sh0_spec = _first_tile_specs(tm, d)
    xn_spec, scn_spec, shn_spec = _next_tile_specs(m, tm, d // tn, d, tps, 1, 0)
    u = pl.pallas_call(
        _glu_kernel,
        out_shape=jax.ShapeDtypeStruct((m, d), F32),
        grid=(m // tm, d // tn),
        in_specs=[x0_spec, xn_spec, _row_spec(d, False), sc0_spec, sh0_spec, scn_spec, shn_spec,
                  pl.BlockSpec((d, tn), lambda i, j: (0, j)),
                  pl.BlockSpec((d, tn), lambda i, j: (0, j + d // tn)),
                  pl.BlockSpec((1, tn), lambda i, j: (0, j)),
                  pl.BlockSpec((1, tn), lambda i, j: (0, j + d // tn))],
        out_specs=pl.BlockSpec((tm, tn), lambda i, j: (i, j)),
        scratch_shapes=[pltpu.VMEM((tm, d), BF16), pltpu.VMEM((tm, d), BF16)],
        compiler_params=_params("arbitrary", "arbitrary"),
        name="conv_glu",
    )(x2, x2, norm_g.reshape(1, d), mod, mod, mod, mod, w_in, w_in,
      p["b_in"].reshape(1, 2 * d), p["b_in"].reshape(1, 2 * d))

    tm = 512
    tps = seq // tm
    hb = tm // CONV_HALO
    return pl.pallas_call(
        functools.partial(_conv_kernel, tm=tm, tiles_per_seq=tps),
        out_shape=jax.ShapeDtypeStruct((m, d), F32),
        grid=(m // tm, d // tn),
        in_specs=[pl.BlockSpec((tm, d), lambda i, j: (i, 0)),
                  pl.BlockSpec((CONV_HALO, d), lambda i, j: (jnp.maximum(i * hb - 1, 0), 0)),
                  pl.BlockSpec((CONV_WIDTH, d), lambda i, j: (0, 0)),
                  _row_spec(d, False), _row_spec(d, False), _row_spec(d, False),
                  pl.BlockSpec((d, tn), lambda i, j: (0, j)),
                  _row_spec(tn, True),
                  pl.BlockSpec((tm, tn), lambda i, j: (i, j)),
                  _mod_spec(2, tps, tn, True)],
        out_specs=pl.BlockSpec((tm, tn), lambda i, j: (i, j)),
        scratch_shapes=[pltpu.VMEM((tm + CONV_HALO, d), F32),
                        pltpu.VMEM((7, tm + CONV_HALO - 8, LANES), F32),
                        pltpu.VMEM((tm, d), F32),
                        pltpu.VMEM((tm, d), BF16)],
        compiler_params=_params("parallel", "arbitrary"),
        name="conv_ln_out",
    )(u, u, p["w_dw"], p["b_dw"].reshape(1, d), p["ln_g"].reshape(1, d), p["ln_b"].reshape(1, d),
      p["w_out"].astype(BF16), p["b_out"].reshape(1, d), x2, mod)


def _pool_kernel(x_ref, xh_ref, g_ref, sc_ref, sh_ref, gate_ref, w_ref, b_ref, scale_ref, o_ref, hbuf,
                 *, tm, tiles_per_seq):
    i = pl.program_id(0)
    d = x_ref.shape[1]
    gd = d // len(POOL_WINDOWS)
    first = (i % tiles_per_seq) == 0
    g, sc, sh = g_ref[...], sc_ref[...], sh_ref[...]
    hbuf[0:POOL_HALO, :] = jnp.where(first, 0.0, _norm_mod(xh_ref[...], g, sc, sh))
    hbuf[POOL_HALO:, :] = _norm_mod(x_ref[...], g, sc, sh)
    tpos = (i % tiles_per_seq) * tm + _iota((tm, 1), 0)
    for gi, win in enumerate(POOL_WINDOWS):
        lo, hi = gi * gd, (gi + 1) * gd
        h = hbuf[POOL_HALO:, lo:hi]
        wsum = h
        for k in range(1, win):
            wsum = wsum + hbuf[POOL_HALO - k:POOL_HALO - k + tm, lo:hi]
        cnt = jnp.minimum(tpos + 1, win).astype(F32)
        dlt = wsum / cnt - h
        y = (_dot(dlt.astype(BF16), w_ref[gi]) + b_ref[:, lo:hi]) * scale_ref[:, lo:hi]
        o_ref[:, lo:hi] = x_ref[:, lo:hi] + gate_ref[:, lo:hi] * y


def _pool_layer(x2, mod, norm_g, p, seq):
    m, d = x2.shape
    tm = 256
    tps = seq // tm
    hb = tm // POOL_HALO
    ng, gd = p["w"].shape[0], p["w"].shape[1]
    return pl.pallas_call(
        functools.partial(_pool_kernel, tm=tm, tiles_per_seq=tps),
        out_shape=jax.ShapeDtypeStruct((m, d), F32),
        grid=(m // tm,),
        in_specs=[pl.BlockSpec((tm, d), lambda i: (i, 0)),
                  pl.BlockSpec((POOL_HALO, d), lambda i: (jnp.maximum(i * hb - 1, 0), 0)),
                  _row_spec(d, False),
                  _mod_spec(1, tps, d, False), _mod_spec(0, tps, d, False), _mod_spec(2, tps, d, False),
                  pl.BlockSpec((ng, gd, gd), lambda i: (0, 0, 0)),
                  _row_spec(d, False), _row_spec(d, False)],
        out_specs=pl.BlockSpec((tm, d), lambda i: (i, 0)),
        scratch_shapes=[pltpu.VMEM((tm + POOL_HALO, d), F32)],
        compiler_params=_params("parallel"),
        name="pool_mixer",
    )(x2, x2, norm_g.reshape(1, d), mod, mod, mod, p["w"].astype(BF16),
      p["b"].reshape(1, d), p["scale"].reshape(1, d))


def _cmp_kernel(x_ref, pe_ref, w1_ref, w2_ref, o_ref, *, n_sub):
    dh = x_ref.shape[1]
    acc_a = jnp.zeros((n_sub, dh), F32)
    acc_b = jnp.zeros((n_sub, dh), F32)
    for p in range(CMP_STRIDE):
        xp = x_ref[pl.ds(p, n_sub, stride=CMP_STRIDE), :]
        acc_a = acc_a + _dot((xp + pe_ref[p:p + 1, :]).astype(BF16), w1_ref[p])
        q = CMP_STRIDE + p
        acc_b = acc_b + _dot((xp + pe_ref[q:q + 1, :]).astype(BF16), w1_ref[q])
    hid = acc_a + pltpu.roll(acc_b, n_sub - 1, 0)
    hid = hid * _sigmoid(hid)
    o_ref[...] = _dot(hid.astype(BF16), w2_ref[...])


def _nsa_compress(kvc3, pe, w1, w2):
    b, s, _ = kvc3.shape
    g, dh = NSA_KV_GROUPS, NSA_HEAD_DIM
    n_sub = s // CMP_STRIDE
    return pl.pallas_call(
        functools.partial(_cmp_kernel, n_sub=n_sub),
        out_shape=jax.ShapeDtypeStruct((2, b, g, n_sub, dh), F32),
        grid=(2, b, g),
        in_specs=[pl.BlockSpec((None, s, dh), lambda kv, bi, gi: (bi, 0, kv * g + gi)),
                  pl.BlockSpec((None, CMP_BLOCK, dh), lambda kv, bi, gi: (kv, 0, 0)),
                  pl.BlockSpec((None, CMP_BLOCK, dh, dh), lambda kv, bi, gi: (kv, 0, 0, 0)),
                  pl.BlockSpec((None, dh, dh), lambda kv, bi, gi: (kv, 0, 0))],
        out_specs=pl.BlockSpec((None, None, None, n_sub, dh), lambda kv, bi, gi: (kv, bi, gi, 0, 0)),
        compiler_params=_params("parallel", "parallel", "parallel"),
        name="nsa_compress",
    )(kvc3, pe, w1.astype(BF16), w2.astype(BF16))


def _sel_block_of(pos):
    assert SEL_BLOCK & (SEL_BLOCK - 1) == 0
    return jnp.right_shift(pos, SEL_BLOCK.bit_length() - 1)


def _split3(x):
    hi = x.astype(BF16)
    r = x - hi.astype(F32)
    mid = r.astype(BF16)
    lo = (r - mid.astype(F32)).astype(BF16)
    return hi, mid, lo


def _nsa_kernel(q_ref, ks_ref, vs_ref, kw_ref, vw_ref, kc_ref, vc_ref, gt_ref, at_ref, et_ref, wb_ref, o_ref,
                *, tq, tk):
    hpg, dh = NSA_HPG, NSA_HEAD_DIM
    assert tq == dh
    ng, n_cmp, _ = kc_ref.shape
    n_sel = at_ref.shape[0]
    rows = hpg * tq
    q0 = pl.program_id(2) * tq
    c = dh ** -0.5 * LOG2E
    tpos = q0 + _iota((tq, 1), 0)
    gcols = lambda g: slice(g * dh, (g + 1) * dh)

    def select(g):
        q = q_ref[:, g * hpg * dh:(g + 1) * hpg * dh]
        q4 = jnp.concatenate([q[:, h * dh:(h + 1) * dh] for h in range(hpg)], axis=0)
        sc = _dot_nt(q4, kc_ref[g].astype(BF16)).reshape(hpg, tq, n_cmp)
        cvalid = (_iota((tq, n_cmp), 1) * CMP_STRIDE + (CMP_BLOCK - 1)) <= tpos
        sc = sc + jnp.where(cvalid, 0.0, NEG)[None]
        e = jnp.exp2((sc - jnp.max(sc, axis=-1, keepdims=True)) * c)
        l = jnp.sum(e, axis=-1, keepdims=True)
        row_ok = jnp.where(tpos >= CMP_BLOCK - 1, 1.0, 0.0)
        p_cmp = e * (row_ok / jnp.maximum(l, 1e-30))
        o_cmp = _dot(p_cmp.reshape(rows, n_cmp).astype(BF16), vc_ref[g].astype(BF16))
        pg = p_cmp[0]
        for h in range(1, hpg):
            pg = pg + p_cmp[h]

        at = at_ref[...]
        imp_t = sum(_dot_nt(at, piece) for piece in _split3(pg))
        blk = _iota((n_sel, tq), 0)
        cur = _sel_block_of(q0 + _iota((n_sel, tq), 1))
        forced = (blk == 0) | (blk == cur) | (blk == cur - 1)
        val = jnp.where(forced, FORCE, jnp.where(blk > cur, NEG, imp_t))
        ranks = []
        for r0 in range(0, n_sel, 8):
            mine = val[r0:r0 + 8, :]
            rank = jnp.zeros((8, tq), F32)
            for mm in range(n_sel):
                other = val[mm:mm + 1, :]
                if mm < r0:
                    rank = rank + jnp.where(other >= mine, 1.0, 0.0)
                elif mm >= r0 + 8:
                    rank = rank + jnp.where(other > mine, 1.0, 0.0)
                else:
                    wins_tie = jnp.where(_iota((8, tq), 0) + r0 > mm, 1.0, 0.0)
                    rank = rank + jnp.where(other > mine, 1.0, jnp.where(other == mine, wins_tie, 0.0))
            ranks.append(rank)
        rank = jnp.concatenate(ranks, axis=0)
        keep = jnp.where(rank < SEL_TOPK, jnp.where(blk <= cur, 0.0, NEG), NEG)
        keep = jnp.concatenate([keep, jnp.zeros((dh - n_sel, tq), F32)], axis=0).T.astype(BF16)
        q_sel = jnp.concatenate([q4, jnp.concatenate([keep] * hpg, axis=0)], axis=1)
        return o_cmp, q4, q_sel

    o_cmp, q4, q_sel = zip(*[select(g) for g in range(ng)])

    def sel_tile(j, carry, last):
        k0 = pl.multiple_of(j * tk, tk)
        onehot = et_ref[pl.ds(k0, tk), :]
        out = []
        for g in range(ng):
            m_i, acc = carry[g]
            k_aug = jnp.concatenate([ks_ref[pl.ds(k0, tk), gcols(g)], onehot], axis=1)
            s = _dot_nt(q_sel[g], k_aug)
            if last:
                causal = (k0 + _iota((tq, tk), 1)) <= tpos
                s = jnp.where(causal[None], s.reshape(hpg, tq, tk), NEG).reshape(rows, tk)
            m_new = jnp.maximum(m_i, jnp.max(s, axis=-1, keepdims=True))
            alpha = jnp.exp2((m_i - m_new) * c)
            e = jnp.exp2((s - m_new) * c).astype(BF16)
            v_aug = jnp.concatenate([vs_ref[pl.ds(k0, tk), gcols(g)], jnp.ones((tk, dh), BF16)], axis=1)
            out.append((m_new, alpha * acc + _dot(e, v_aug)))
        return tuple(out)

    j_last = q0 // tk
    init = tuple((jnp.full((rows, 1), NEG, F32), jnp.zeros((rows, 2 * dh), F32)) for _ in range(ng))
    carry = lax.fori_loop(0, j_last // 2,
                          lambda p, cr: sel_tile(2 * p + 1, sel_tile(2 * p, cr, False), False), init)
    carry = lax.fori_loop(j_last // 2 * 2, j_last, lambda j, cr: sel_tile(j, cr, False), carry)
    carry = sel_tile(j_last, carry, True)

    band = WINDOW + tq
    k0w = pl.multiple_of(jnp.maximum(q0 - WINDOW, 0), tq)
    eye = jnp.where(_iota((rows, dh), 1) == (_iota((rows, dh), 0) & (tq - 1)), 1.0, 0.0).astype(BF16)
    for g in range(ng):
        acc = carry[g][1]
        o_sel = acc[:, :dh] / jnp.maximum(acc[:, dh:], 1e-30)
        q_win = jnp.concatenate([q4[g], eye], axis=1)
        k_aug = jnp.concatenate([kw_ref[pl.ds(k0w, band), gcols(g)], wb_ref[...]], axis=1)
        sw = _dot_nt(q_win, k_aug)
        e = jnp.exp2((sw - jnp.max(sw, axis=-1, keepdims=True)) * c).astype(BF16)
        v_aug = jnp.concatenate([vw_ref[pl.ds(k0w, band), gcols(g)], jnp.ones((band, dh), BF16)], axis=1)
        pv = _dot(e, v_aug)
        o_win = pv[:, :dh] / jnp.maximum(pv[:, dh:], 1e-30)

        gt = gt_ref[g]
        for h in range(hpg):
            g_cmp, g_sel, g_win = (gt[:, 3 * h + r:3 * h + r + 1] for r in range(3))
            hs = slice(h * tq, (h + 1) * tq)
            o_ref[:, (g * hpg + h) * dh:(g * hpg + h + 1) * dh] = (
                g_cmp * o_cmp[g][hs] + g_sel * o_sel[hs] + g_win * o_win[hs]).astype(o_ref.dtype)


def _nsa_attention(proj, kv_cmp, gates, seq):
    b = proj.shape[0]
    g, hpg, dh = NSA_KV_GROUPS, NSA_HPG, NSA_HEAD_DIM
    tq, tk = 128, 512
    n_cmp = kv_cmp.shape[3]
    n_sel = seq // SEL_BLOCK
    d_q = g * hpg * dh
    col_q, col_slc, col_win = 0, d_q // dh, (d_q + 2 * g * dh) // dh

    sub = jnp.arange(n_cmp)[None, :] // (SEL_BLOCK // CMP_STRIDE)
    nxt = (jnp.arange(n_cmp)[None, :] + 1) // (SEL_BLOCK // CMP_STRIDE)
    rows = jnp.arange(n_sel)[:, None]
    agg = ((sub == rows).astype(F32) + (nxt == rows).astype(F32)).astype(BF16)
    assert n_sel <= dh
    block_onehot = (jnp.arange(seq)[:, None] // SEL_BLOCK == jnp.arange(dh)[None, :]).astype(BF16)
    band = WINDOW + tq
    n_pat = WINDOW // tq + 1
    pat_q0 = jnp.arange(n_pat)[:, None, None] * tq
    key = jnp.maximum(pat_q0 - WINDOW, 0) + jnp.arange(band)[None, :, None]
    qry = pat_q0 + jnp.arange(tq)[None, None, :]
    win_bias = jnp.where((key <= qry) & (key > qry - WINDOW), 0.0, NEG).astype(BF16)

    ng = NSA_GROUPS_PER_STEP
    assert g % ng == 0 and col_slc % ng == 0 and col_win % ng == 0
    kv_spec = lambda col: pl.BlockSpec((None, seq, ng * dh), lambda bi, gi, i: (bi, 0, col // ng + gi))
    cmp_spec = lambda kv: pl.BlockSpec((None, None, ng, n_cmp, dh), lambda bi, gi, i: (kv, bi, gi, 0, 0))
    return pl.pallas_call(
        functools.partial(_nsa_kernel, tq=tq, tk=tk),
        out_shape=jax.ShapeDtypeStruct((b, seq, d_q), BF16),
        grid=(b, g // ng, seq // tq),
        in_specs=[pl.BlockSpec((None, tq, ng * hpg * dh), lambda bi, gi, i: (bi, i, col_q + gi)),
                  kv_spec(col_slc), kv_spec(col_slc + g), kv_spec(col_win), kv_spec(col_win + g),
                  cmp_spec(0), cmp_spec(1),
                  pl.BlockSpec((None, ng, tq, 3 * hpg), lambda bi, gi, i: (bi, gi, i, 0)),
                  pl.BlockSpec((n_sel, n_cmp), lambda bi, gi, i: (0, 0)),
                  pl.BlockSpec((seq, dh), lambda bi, gi, i: (0, 0)),
                  pl.BlockSpec((None, band, tq), lambda bi, gi, i: (jnp.minimum(i, n_pat - 1), 0, 0))],
        out_specs=pl.BlockSpec((None, tq, ng * hpg * dh), lambda bi, gi, i: (bi, i, gi)),
        compiler_params=_params("parallel", "parallel", "arbitrary"),
        name="nsa_attention",
    )(proj, proj, proj, proj, proj, kv_cmp, kv_cmp, gates, agg, block_onehot, win_bias)


def _nsa_layer(x2, mod, norm_g, p, batch, seq):
    m, d = x2.shape
    g, hpg, dh = NSA_KV_GROUPS, NSA_HPG, NSA_HEAD_DIM
    w_a = jnp.concatenate([p["w_q"], p["w_kv_slc"], p["w_kv_win"]], axis=1).astype(BF16)
    n_gate = p["w_gate"].shape[1]
    gate_pad = 128 - n_gate
    w_b = jnp.concatenate([p["w_kv_cmp"], jnp.pad(p["w_gate"], ((0, 0), (0, gate_pad)))], axis=1).astype(BF16)
    n_cmp_cols = p["w_kv_cmp"].shape[1]
    b_b = jnp.concatenate([jnp.zeros((n_cmp_cols,), F32), jnp.pad(p["b_gate"], (0, gate_pad))])
    proj = _nm_mm(x2, mod, norm_g, w_a, jnp.zeros((w_a.shape[1],), F32), seq, BF16, tn=1024)
    proj_b = _nm_mm(x2, mod, norm_g, w_b, b_b, seq, F32, tn=w_b.shape[1], sig_from=n_cmp_cols)
    proj_b3 = proj_b.reshape(batch, seq, proj_b.shape[1])
    kv_cmp = _nsa_compress(proj_b3, p["cmp_pe"], p["cmp_w1"], p["cmp_w2"])
    gates = proj_b3[:, :, n_cmp_cols:n_cmp_cols + n_gate].reshape(batch, seq, g, 3 * hpg)
    gates = jnp.transpose(gates, (0, 2, 1, 3))
    o = _nsa_attention(proj.reshape(batch, seq, proj.shape[1]), kv_cmp, gates, seq)
    return _mm_res(o.reshape(m, d), p["w_o"].astype(BF16), x2, mod, seq)


def _rope(x, cos2, sin2):
    half = x.shape[1] // 2
    swapped = jnp.concatenate([x[:, half:], x[:, :half]], axis=-1)
    return x * cos2 + swapped * sin2


def _mla_proj_kernel(x_ref, g_ref, sc_ref, sh_ref, w_ref, qg_ref, kvg_ref, cos_ref, sin_ref,
                     wuq_ref, wuk_ref, wuv_ref, q_ref, k_ref, v_ref):
    h = _norm_mod(x_ref[...], g_ref[...], sc_ref[...], sh_ref[...]).astype(BF16)
    y = _dot(h, w_ref[...])
    r0, r1, r2 = MLA_Q_RANK, MLA_Q_RANK + MLA_KV_RANK, MLA_Q_RANK + MLA_KV_RANK + MLA_ROPE_DIM
    cos2, sin2 = cos_ref[...], sin_ref[...]
    cq = _rms(y[:, :r0], qg_ref[...]).astype(BF16)
    ckv = _rms(y[:, r0:r1], kvg_ref[...]).astype(BF16)
    kr = _rope(y[:, r1:r2], cos2, sin2).astype(BF16)
    for hd in range(q_ref.shape[0]):
        yq = _dot(cq, wuq_ref[hd])
        qr = _rope(yq[:, MLA_NOPE_DIM:], cos2, sin2)
        q_ref[hd] = jnp.concatenate([yq[:, :MLA_NOPE_DIM], qr], axis=-1).astype(BF16)
        k_ref[hd] = jnp.concatenate([_dot(ckv, wuk_ref[hd]).astype(BF16), kr], axis=-1)
        v_ref[hd] = _dot(ckv, wuv_ref[hd]).astype(BF16)


def _mla_attn_kernel(q_ref, k_ref, v_ref, o_ref, *, tq, tk):
    dqk = q_ref.shape[2]
    nh, _, dv = v_ref.shape
    assert tq == tk
    c = dqk ** -0.5 * LOG2E
    ones = jnp.ones((tk, dv), BF16)

    def tile(j, carry, diagonal):
        k0 = pl.multiple_of(j * tk, tk)
        out = []
        for h in range(nh):
            m_i, acc = carry[h]
            s = _dot_nt(q_ref[h], k_ref[h, pl.ds(k0, tk), :])
            if diagonal:
                s = jnp.where(_iota((tq, tk), 1) <= _iota((tq, tk), 0), s, NEG)
            m_new = jnp.maximum(m_i, jnp.max(s, axis=-1, keepdims=True))
            alpha = jnp.exp2((m_i - m_new) * c)
            e = jnp.exp2((s - m_new) * c).astype(BF16)
            v_aug = jnp.concatenate([v_ref[h, pl.ds(k0, tk), :], ones], axis=1)
            out.append((m_new, alpha * acc + _dot(e, v_aug)))
        return tuple(out)

    i = pl.program_id(2)
    init = tuple((jnp.full((tq, 1), NEG, F32), jnp.zeros((tq, 2 * dv), F32)) for _ in range(nh))
    carry = lax.fori_loop(0, i // 2, lambda p, cr: tile(2 * p + 1, tile(2 * p, cr, False), False), init)
    carry = lax.fori_loop(i // 2 * 2, i, lambda j, cr: tile(j, cr, False), carry)
    carry = tile(i, carry, True)
    for h in range(nh):
        acc = carry[h][1]
        o_ref[:, h * dv:(h + 1) * dv] = (acc[:, :dv] / jnp.maximum(acc[:, dv:], 1e-30)).astype(o_ref.dtype)


def _mla_layer(x2, mod, norm_g, p, batch, seq):
    m, d = x2.shape
    hh, dn, dr, dv = MLA_HEADS, MLA_NOPE_DIM, MLA_ROPE_DIM, MLA_V_DIM
    rq, rkv = MLA_Q_RANK, MLA_KV_RANK
    tm = 512
    tps = seq // tm

    pos = jnp.arange(seq, dtype=F32)
    inv_freq = ROPE_THETA ** (-jnp.arange(0, dr, 2, dtype=F32) / dr)
    ang = pos[:, None] * inv_freq[None, :]
    cos, sin = jnp.cos(ang), jnp.sin(ang)
    cos2 = jnp.concatenate([cos, cos], axis=-1)
    sin2 = jnp.concatenate([-sin, sin], axis=-1)
    rope_spec = pl.BlockSpec((tm, dr), lambda i, *_: (i % tps, 0))

    n_down = rq + rkv + dr
    n_pad = -n_down % 128
    w_down = jnp.pad(jnp.concatenate([p["w_dq"], p["w_dkv"]], axis=1), ((0, 0), (0, n_pad))).astype(BF16)
    dqk = dn + dr
    w_uq = jnp.transpose(p["w_uq"].reshape(rq, hh, dqk), (1, 0, 2)).astype(BF16)
    w_uk = jnp.transpose(p["w_uk"].reshape(rkv, hh, dn), (1, 0, 2)).astype(BF16)
    w_uv = jnp.transpose(p["w_uv"].reshape(rkv, hh, dv), (1, 0, 2)).astype(BF16)
    whole = lambda a: pl.BlockSpec(a.shape, lambda i: (0,) * a.ndim)
    head_out = lambda width: pl.BlockSpec((None, hh, tm, width), lambda i: (i // tps, 0, i % tps, 0))
    qf, kf, vf = pl.pallas_call(
        _mla_proj_kernel,
        out_shape=(jax.ShapeDtypeStruct((batch, hh, seq, dqk), BF16),
                   jax.ShapeDtypeStruct((batch, hh, seq, dqk), BF16),
                   jax.ShapeDtypeStruct((batch, hh, seq, dv), BF16)),
        grid=(m // tm,),
        in_specs=[pl.BlockSpec((tm, d), lambda i: (i, 0)),
                  _row_spec(d, False),
                  _mod_spec(1, tps, d, False), _mod_spec(0, tps, d, False),
                  whole(w_down), _row_spec(rq, False), _row_spec(rkv, False), rope_spec, rope_spec,
                  whole(w_uq), whole(w_uk), whole(w_uv)],
        out_specs=(head_out(dqk), head_out(dqk), head_out(dv)),
        compiler_params=_params("parallel"),
        name="mla_proj",
    )(x2, norm_g.reshape(1, d), mod, mod, w_down, p["q_norm_g"].reshape(1, rq),
      p["kv_norm_g"].reshape(1, rkv), cos2, sin2, w_uq, w_uk, w_uv)

    tq = tk = 512
    nh = 4
    o = pl.pallas_call(
        functools.partial(_mla_attn_kernel, tq=tq, tk=tk),
        out_shape=jax.ShapeDtypeStruct((batch, seq, hh * dv), BF16),
        grid=(batch, hh // nh, seq // tq),
        in_specs=[pl.BlockSpec((None, nh, tq, dqk), lambda b, h, i: (b, h, i, 0)),
                  pl.BlockSpec((None, nh, seq, dqk), lambda b, h, i: (b, h, 0, 0)),
                  pl.BlockSpec((None, nh, seq, dv), lambda b, h, i: (b, h, 0, 0))],
        out_specs=pl.BlockSpec((None, tq, nh * dv), lambda b, h, i: (b, i, h)),
        compiler_params=_params("parallel", "parallel", "arbitrary"),
        name="mla_attention",
    )(qf, kf, vf)
    return _mm_res(o.reshape(m, hh * dv), p["w_o"].astype(BF16), x2, mod, seq)


def kernel(x, c, ada_w, ada_b, norm1_g, norm2_g, mlp_w1, mlp_w2, final_g, conv_w_in, conv_b_in, conv_w_dw, conv_b_dw, conv_ln_g, conv_ln_b, conv_w_out, conv_b_out, nsa_w_q, nsa_w_kv_cmp, nsa_w_kv_slc, nsa_w_kv_win, nsa_cmp_pe, nsa_cmp_w1, nsa_cmp_w2, nsa_w_gate, nsa_b_gate, nsa_w_o, pool_w, pool_b, pool_scale, mla_w_dq, mla_q_norm_g, mla_w_uq, mla_w_dkv, mla_kv_norm_g, mla_w_uk, mla_w_uv, mla_w_o):
    batch, seq, d = x.shape
    depth = ada_w.shape[0]
    n_mixers = 4
    mods = _ada_mod(c, ada_w, ada_b)
    x2 = x.reshape(batch * seq, d)
    w1_all, w2_all = mlp_w1.astype(BF16), mlp_w2.astype(BF16)
    for i in range(depth):
        kind, u = i % n_mixers, i // n_mixers
        mod = mods[i].reshape(batch * 6, 1, d)
        if kind == 0:
            p = dict(w_in=conv_w_in[u], b_in=conv_b_in[u], w_dw=conv_w_dw[u], b_dw=conv_b_dw[u],
                     ln_g=conv_ln_g[u], ln_b=conv_ln_b[u], w_out=conv_w_out[u], b_out=conv_b_out[u])
            x2 = _conv_layer(x2, mod, norm1_g[i], p, seq)
        elif kind == 1:
            p = dict(w_q=nsa_w_q[u], w_kv_cmp=nsa_w_kv_cmp[u], w_kv_slc=nsa_w_kv_slc[u],
                     w_kv_win=nsa_w_kv_win[u], cmp_pe=nsa_cmp_pe[u], cmp_w1=nsa_cmp_w1[u],
                     cmp_w2=nsa_cmp_w2[u], w_gate=nsa_w_gate[u], b_gate=nsa_b_gate[u], w_o=nsa_w_o[u])
            x2 = _nsa_layer(x2, mod, norm1_g[i], p, batch, seq)
        elif kind == 2:
            p = dict(w=pool_w[u], b=pool_b[u], scale=pool_scale[u])
            x2 = _pool_layer(x2, mod, norm1_g[i], p, seq)
        else:
            p = dict(w_dq=mla_w_dq[u], q_norm_g=mla_q_norm_g[u], w_uq=mla_w_uq[u], w_dkv=mla_w_dkv[u],
                     kv_norm_g=mla_kv_norm_g[u], w_uk=mla_w_uk[u], w_uv=mla_w_uv[u], w_o=mla_w_o[u])
            x2 = _mla_layer(x2, mod, norm1_g[i], p, batch, seq)
        x2 = _mlp(x2, mod, norm2_g[i], w1_all, w2_all, i, seq, final_g=final_g if i == depth - 1 else None)
    return x2.reshape(batch, seq, d)
```

```python
import functools

import jax
import jax.numpy as jnp
from jax import lax
from jax.experimental import pallas as pl
from jax.experimental.pallas import tpu as pltpu

F32 = jnp.float32
BF16 = jnp.bfloat16

EPS = 1e-6
NEG = -1e30
FORCE = 1e30
LOG2E = 1.4426950408889634

CONV_WIDTH = 31
CONV_HALO = 32
NSA_HEADS = 16
NSA_HEAD_DIM = 128
NSA_KV_GROUPS = 4
NSA_HPG = NSA_HEADS // NSA_KV_GROUPS
NSA_GROUPS_PER_STEP = 4
CMP_BLOCK = 32
CMP_STRIDE = 16
SEL_BLOCK = 64
SEL_TOPK = 16
WINDOW = 512
POOL_WINDOWS = (2, 4, 8, 16)
POOL_HALO = 16
MLA_HEADS = 16
MLA_NOPE_DIM = 128
MLA_ROPE_DIM = 64
MLA_V_DIM = 128
MLA_Q_RANK = 512
MLA_KV_RANK = 256
ROPE_THETA = 10000.0

VMEM_LIMIT_BYTES = 56 * 1024 * 1024
LANES = 128
SUBLANES = 8

ROW_TILE = 512
MLP_FF_TILE = 1024
ADA_COL_TILE = 1024
PROJ_TILE = (1024, 1024)
GLU_TILE = (1024, 512)
CONV_TILE = (512, 1024)
POOL_ROW_TILE = 256
NSA_Q_TILE, NSA_KEY_TILE = 128, 512
MLA_SEQ_TILE = 512
MLA_HEADS_PER_STEP = 4


def _params(*semantics):
    return pltpu.CompilerParams(dimension_semantics=semantics, vmem_limit_bytes=VMEM_LIMIT_BYTES)


def _dot(a, b):
    return jnp.dot(a, b, preferred_element_type=F32)


def _dot_nt(a, b):
    return lax.dot_general(a, b, (((1,), (1,)), ((), ())), preferred_element_type=F32)


def _rms(x, g):
    return x * lax.rsqrt(jnp.mean(x * x, axis=-1, keepdims=True) + EPS) * g


def _norm_mod(x, g, sc, sh):
    return _rms(x, g) * (1.0 + sc) + sh


def _sigmoid(x):
    return 1.0 / (1.0 + jnp.exp(-x))


def _iota(shape, dim):
    return lax.broadcasted_iota(jnp.int32, shape, dim)


def _mod_spec(k, tiles_per_seq, width, tiled):
    if tiled:
        return pl.BlockSpec((None, 1, width), lambda i, j: ((i // tiles_per_seq) * 6 + k, 0, j))
    return pl.BlockSpec((None, 1, width), lambda i, *_: ((i // tiles_per_seq) * 6 + k, 0, 0))


def _row_spec(width, tiled):
    if tiled:
        return pl.BlockSpec((1, width), lambda i, j: (0, j))
    return pl.BlockSpec((1, width), lambda i, *_: (0, 0))


def _ada_kernel(c_ref, w_ref, b_ref, o_ref):
    c = c_ref[...]
    cs = c * _sigmoid(c)
    o_ref[...] = _dot(cs.astype(BF16), w_ref[...].astype(BF16)) + b_ref[...]


def _ada_mod(c, ada_w, ada_b):
    depth, d, n = ada_w.shape
    b = c.shape[0]
    bp = SUBLANES
    tn = ADA_COL_TILE
    cp = jnp.pad(c, ((0, bp - b), (0, 0)))
    out = pl.pallas_call(
        _ada_kernel,
        out_shape=jax.ShapeDtypeStruct((depth, bp, n), F32),
        grid=(depth, n // tn),
        in_specs=[pl.BlockSpec((bp, d), lambda l, j: (0, 0)),
                  pl.BlockSpec((None, d, tn), lambda l, j: (l, 0, j)),
                  pl.BlockSpec((None, 1, tn), lambda l, j: (l, 0, j))],
        out_specs=pl.BlockSpec((None, bp, tn), lambda l, j: (l, 0, j)),
        compiler_params=_params("parallel", "parallel"),
        name="ada_mod",
    )(cp, ada_w, ada_b.reshape(depth, 1, n))
    return out[:, :b]


def _with_next_tile_norm(x0_ref, xn_ref, g_ref, sc_ref, sh_ref, scn_ref, shn_ref, h_even, h_odd, body):
    i, j = pl.program_id(0), pl.program_id(1)
    rows = xn_ref.shape[0]

    @pl.when((i == 0) & (j == 0))
    def _():
        h_even[...] = _norm_mod(x0_ref[...], g_ref[...], sc_ref[...], sh_ref[...]).astype(BF16)

    def step(h_cur, h_next):
        body(h_cur)
        r0 = pl.multiple_of(j * rows, rows)
        h_next[pl.ds(r0, rows), :] = _norm_mod(
            xn_ref[...], g_ref[...], scn_ref[...], shn_ref[...]).astype(BF16)

    @pl.when(i % 2 == 0)
    def _():
        step(h_even, h_odd)

    @pl.when(i % 2 == 1)
    def _():
        step(h_odd, h_even)


def _next_tile_specs(m, tm, steps, d, tiles_per_seq, k_scale, k_shift):
    n_tiles = m // tm
    rows = tm // steps
    assert rows * steps == tm and rows % 16 == 0
    nxt = lambda i: jnp.minimum(i + 1, n_tiles - 1)
    mod_next = lambda k: pl.BlockSpec((None, 1, d), lambda i, j: ((nxt(i) // tiles_per_seq) * 6 + k, 0, 0))
    return (pl.BlockSpec((rows, d), lambda i, j: (nxt(i) * steps + j, 0)), mod_next(k_scale), mod_next(k_shift))


def _mlp_kernel(x_ref, xn_ref, g_ref, sc_ref, sh_ref, scn_ref, shn_ref, gate_ref, w1_ref, w2_ref, *rest, final):
    if final:
        fg_ref, o_ref, h_even, h_odd, acc_ref = rest
    else:
        o_ref, h_even, h_odd, acc_ref = rest
    f = pl.program_id(1)

    @pl.when(f == 0)
    def _():
        acc_ref[...] = jnp.zeros_like(acc_ref)

    def body(h_cur):
        a = _dot(h_cur[...], w1_ref[...])
        a = jnp.square(jnp.maximum(a, 0.0)).astype(BF16)
        acc_ref[...] += _dot(a, w2_ref[...])

    _with_next_tile_norm(x_ref, xn_ref, g_ref, sc_ref, sh_ref, scn_ref, shn_ref, h_even, h_odd, body)

    @pl.when(f == pl.num_programs(1) - 1)
    def _():
        out = x_ref[...] + gate_ref[...] * acc_ref[...]
        if final:
            out = _rms(out, fg_ref[...])
        o_ref[...] = out


def _mlp(x2, mod, norm_g, w1, w2, layer, seq, final_g=None):
    m, d = x2.shape
    dff = w1.shape[2]
    tm, tf = ROW_TILE, MLP_FF_TILE
    tps = seq // tm
    final = final_g is not None
    nt, nf = m // tm, dff // tf
    xn_spec, scn_spec, shn_spec = _next_tile_specs(m, tm, nf, d, tps, 4, 3)
    in_specs = [pl.BlockSpec((tm, d), lambda i, f: (i, 0)),
                xn_spec,
                _row_spec(d, False),
                _mod_spec(4, tps, d, False), _mod_spec(3, tps, d, False), scn_spec, shn_spec,
                _mod_spec(5, tps, d, False),
                pl.BlockSpec((None, d, tf), lambda i, f: (layer, 0, f)),
                pl.BlockSpec((None, tf, d), lambda i, f: (layer, f, 0))]
    args = [x2, x2, norm_g.reshape(1, d), mod, mod, mod, mod, mod, w1, w2]
    if final:
        in_specs.append(_row_spec(d, False))
        args.append(final_g.reshape(1, d))
    return pl.pallas_call(
        functools.partial(_mlp_kernel, final=final),
        out_shape=jax.ShapeDtypeStruct((m, d), F32),
        grid=(nt, nf),
        in_specs=in_specs,
        out_specs=pl.BlockSpec((tm, d), lambda i, f: (i, 0)),
        scratch_shapes=[pltpu.VMEM((tm, d), BF16), pltpu.VMEM((tm, d), BF16), pltpu.VMEM((tm, d), F32)],
        compiler_params=_params("arbitrary", "arbitrary"),
        name="mlp",
    )(*args)


def _nm_mm_kernel(x0_ref, xn_ref, g_ref, sc_ref, sh_ref, scn_ref, shn_ref, w_ref, b_ref, o_ref, h_even, h_odd,
                  *, sig_from, tn):
    def body(h_cur):
        y = _dot(h_cur[...], w_ref[...]) + b_ref[...]
        if sig_from is not None:
            col = pl.program_id(1) * tn + _iota(y.shape, 1)
            y = jnp.where(col >= sig_from, _sigmoid(y), y)
        o_ref[...] = y.astype(o_ref.dtype)

    _with_next_tile_norm(x0_ref, xn_ref, g_ref, sc_ref, sh_ref, scn_ref, shn_ref, h_even, h_odd, body)


def _first_tile_specs(tm, d):
    first_mod = lambda k: pl.BlockSpec((None, 1, d), lambda i, j: (k, 0, 0))
    return pl.BlockSpec((tm, d), lambda i, j: (0, 0)), first_mod(1), first_mod(0)


def _nm_mm(x2, mod, norm_g, w, bias, seq, out_dtype, tm, tn, sig_from=None):
    m, d = x2.shape
    n = w.shape[1]
    tps = seq // tm
    x0_spec, sc0_spec, sh0_spec = _first_tile_specs(tm, d)
    xn_spec, scn_spec, shn_spec = _next_tile_specs(m, tm, n // tn, d, tps, 1, 0)
    return pl.pallas_call(
        functools.partial(_nm_mm_kernel, sig_from=sig_from, tn=tn),
        out_shape=jax.ShapeDtypeStruct((m, n), out_dtype),
        grid=(m // tm, n // tn),
        in_specs=[x0_spec, xn_spec, _row_spec(d, False), sc0_spec, sh0_spec, scn_spec, shn_spec,
                  pl.BlockSpec((d, tn), lambda i, j: (0, j)),
                  _row_spec(tn, True)],
        out_specs=pl.BlockSpec((tm, tn), lambda i, j: (i, j)),
        scratch_shapes=[pltpu.VMEM((tm, d), BF16), pltpu.VMEM((tm, d), BF16)],
        compiler_params=_params("arbitrary", "arbitrary"),
        name="norm_mod_matmul",
    )(x2, x2, norm_g.reshape(1, d), mod, mod, mod, mod, w, bias.reshape(1, n))


def _mm_res_kernel(a_ref, w_ref, x_ref, gate_ref, o_ref):
    o_ref[...] = x_ref[...] + gate_ref[...] * _dot(a_ref[...], w_ref[...])


def _mm_res(a, w, x2, mod, seq):
    m, k = a.shape
    n = w.shape[1]
    tm = ROW_TILE
    tps = seq // tm
    return pl.pallas_call(
        _mm_res_kernel,
        out_shape=jax.ShapeDtypeStruct((m, n), F32),
        grid=(m // tm,),
        in_specs=[pl.BlockSpec((tm, k), lambda i: (i, 0)),
                  pl.BlockSpec((k, n), lambda i: (0, 0)),
                  pl.BlockSpec((tm, n), lambda i: (i, 0)),
                  _mod_spec(2, tps, n, False)],
        out_specs=pl.BlockSpec((tm, n), lambda i: (i, 0)),
        compiler_params=_params("parallel"),
        name="matmul_residual",
    )(a, w, x2, mod)


def _glu_kernel(x0_ref, xn_ref, g_ref, sc_ref, sh_ref, scn_ref, shn_ref, wa_ref, wg_ref, ba_ref, bg_ref,
                o_ref, h_even, h_odd):
    def body(h_cur):
        h = h_cur[...]
        a = _dot(h, wa_ref[...]) + ba_ref[...]
        gt = _dot(h, wg_ref[...]) + bg_ref[...]
        o_ref[...] = a * _sigmoid(gt)

    _with_next_tile_norm(x0_ref, xn_ref, g_ref, sc_ref, sh_ref, scn_ref, shn_ref, h_even, h_odd, body)


def _conv_kernel(u_ref, uh_ref, wdw_ref, bdw_ref, lng_ref, lnb_ref, wo_ref, bo_ref, x_ref, gate_ref,
                 o_ref, ubuf, shifted, conv_ref, v_ref, *, tm, tiles_per_seq):
    i = pl.program_id(0)
    d = u_ref.shape[1]
    rc = 64
    rn = 16
    span = tm + CONV_HALO - 8

    @pl.when(pl.program_id(1) == 0)
    def _():
        first = (i % tiles_per_seq) == 0
        ubuf[0:CONV_HALO, :] = jnp.where(first, 0.0, uh_ref[...])
        ubuf[CONV_HALO:, :] = u_ref[...]

        def column(ct, carry):
            cs = pl.ds(pl.multiple_of(ct * LANES, LANES), LANES)
            for r in range(1, 8):
                shifted[r - 1] = ubuf[r:r + span, cs]
            taps = [jnp.broadcast_to(wdw_ref[k:k + 1, cs], (8, LANES)) for k in range(CONV_WIDTH)]
            bias = jnp.broadcast_to(bdw_ref[:, cs], (8, LANES))

            def chunk(c, inner):
                r0 = pl.multiple_of(c * rc, rc)
                for v in range(rc // 8):
                    acc = bias
                    for k in range(CONV_WIDTH):
                        off = CONV_HALO - (CONV_WIDTH - 1) + k
                        q, r = divmod(off, 8)
                        start = pl.multiple_of(r0 + 8 * (q + v), 8)
                        rows = ubuf[pl.ds(start, 8), cs] if r == 0 else shifted[r - 1, pl.ds(start, 8), :]
                        acc = acc + taps[k] * rows
                    conv_ref[pl.ds(pl.multiple_of(r0 + 8 * v, 8), 8), cs] = acc
                return inner

            lax.fori_loop(0, tm // rc, chunk, 0)
            return carry

        lax.fori_loop(0, d // LANES, column, 0)

        def norm_chunk(c, carry):
            r0 = pl.multiple_of(c * rn, rn)
            acc = conv_ref[pl.ds(r0, rn), :]
            mu = jnp.mean(acc, axis=-1, keepdims=True)
            cen = acc - mu
            var = jnp.mean(cen * cen, axis=-1, keepdims=True)
            y = cen * lax.rsqrt(var + EPS) * lng_ref[...] + lnb_ref[...]
            v_ref[pl.ds(r0, rn), :] = (y * _sigmoid(y)).astype(BF16)
            return carry

        lax.fori_loop(0, tm // rn, norm_chunk, 0, unroll=4)

    y = _dot(v_ref[...], wo_ref[...]) + bo_ref[...]
    o_ref[...] = x_ref[...] + gate_ref[...] * y


def _conv_layer(x2, mod, norm_g, p, seq):
    m, d = x2.shape
    tm, tn = GLU_TILE
    tps = seq // tm
    w_in = p["w_in"].astype(BF16)
    x0_spec, sc0_spec, sh0_spec = _first_tile_specs(tm, d)
    xn_spec, scn_spec, shn_spec = _next_tile_specs(m, tm, d // tn, d, tps, 1, 0)
    u = pl.pallas_call(
        _glu_kernel,
        out_shape=jax.ShapeDtypeStruct((m, d), F32),
        grid=(m // tm, d // tn),
        in_specs=[x0_spec, xn_spec, _row_spec(d, False), sc0_spec, sh0_spec, scn_spec, shn_spec,
                  pl.BlockSpec((d, tn), lambda i, j: (0, j)),
                  pl.BlockSpec((d, tn), lambda i, j: (0, j + d // tn)),
                  pl.BlockSpec((1, tn), lambda i, j: (0, j)),
                  pl.BlockSpec((1, tn), lambda i, j: (0, j + d // tn))],
        out_specs=pl.BlockSpec((tm, tn), lambda i, j: (i, j)),
        scratch_shapes=[pltpu.VMEM((tm, d), BF16), pltpu.VMEM((tm, d), BF16)],
        compiler_params=_params("arbitrary", "arbitrary"),
        name="conv_glu",
    )(x2, x2, norm_g.reshape(1, d), mod, mod, mod, mod, w_in, w_in,
      p["b_in"].reshape(1, 2 * d), p["b_in"].reshape(1, 2 * d))

    tm, tn = CONV_TILE
    tps = seq // tm
    hb = tm // CONV_HALO
    return pl.pallas_call(
        functools.partial(_conv_kernel, tm=tm, tiles_per_seq=tps),
        out_shape=jax.ShapeDtypeStruct((m, d), F32),
        grid=(m // tm, d // tn),
        in_specs=[pl.BlockSpec((tm, d), lambda i, j: (i, 0)),
                  pl.BlockSpec((CONV_HALO, d), lambda i, j: (jnp.maximum(i * hb - 1, 0), 0)),
                  pl.BlockSpec((CONV_WIDTH, d), lambda i, j: (0, 0)),
                  _row_spec(d, False), _row_spec(d, False), _row_spec(d, False),
                  pl.BlockSpec((d, tn), lambda i, j: (0, j)),
                  _row_spec(tn, True),
                  pl.BlockSpec((tm, tn), lambda i, j: (i, j)),
                  _mod_spec(2, tps, tn, True)],
        out_specs=pl.BlockSpec((tm, tn), lambda i, j: (i, j)),
        scratch_shapes=[pltpu.VMEM((tm + CONV_HALO, d), F32),
                        pltpu.VMEM((7, tm + CONV_HALO - 8, LANES), F32),
                        pltpu.VMEM((tm, d), F32),
                        pltpu.VMEM((tm, d), BF16)],
        compiler_params=_params("parallel", "arbitrary"),
        name="conv_ln_out",
    )(u, u, p["w_dw"], p["b_dw"].reshape(1, d), p["ln_g"].reshape(1, d), p["ln_b"].reshape(1, d),
      p["w_out"].astype(BF16), p["b_out"].reshape(1, d), x2, mod)


def _pool_kernel(x_ref, xh_ref, g_ref, sc_ref, sh_ref, gate_ref, w_ref, b_ref, scale_ref, o_ref, hbuf,
                 *, tm, tiles_per_seq):
    i = pl.program_id(0)
    d = x_ref.shape[1]
    gd = d // len(POOL_WINDOWS)
    first = (i % tiles_per_seq) == 0
    g, sc, sh = g_ref[...], sc_ref[...], sh_ref[...]
    hbuf[0:POOL_HALO, :] = jnp.where(first, 0.0, _norm_mod(xh_ref[...], g, sc, sh))
    hbuf[POOL_HALO:, :] = _norm_mod(x_ref[...], g, sc, sh)
    tpos = (i % tiles_per_seq) * tm + _iota((tm, 1), 0)
    for gi, win in enumerate(POOL_WINDOWS):
        lo, hi = gi * gd, (gi + 1) * gd
        h = hbuf[POOL_HALO:, lo:hi]
        wsum = h
        for k in range(1, win):
            wsum = wsum + hbuf[POOL_HALO - k:POOL_HALO - k + tm, lo:hi]
        cnt = jnp.minimum(tpos + 1, win).astype(F32)
        dlt = wsum / cnt - h
        y = (_dot(dlt.astype(BF16), w_ref[gi]) + b_ref[:, lo:hi]) * scale_ref[:, lo:hi]
        o_ref[:, lo:hi] = x_ref[:, lo:hi] + gate_ref[:, lo:hi] * y


def _pool_layer(x2, mod, norm_g, p, seq):
    m, d = x2.shape
    tm = POOL_ROW_TILE
    tps = seq // tm
    hb = tm // POOL_HALO
    ng, gd = p["w"].shape[0], p["w"].shape[1]
    return pl.pallas_call(
        functools.partial(_pool_kernel, tm=tm, tiles_per_seq=tps),
        out_shape=jax.ShapeDtypeStruct((m, d), F32),
        grid=(m // tm,),
        in_specs=[pl.BlockSpec((tm, d), lambda i: (i, 0)),
                  pl.BlockSpec((POOL_HALO, d), lambda i: (jnp.maximum(i * hb - 1, 0), 0)),
                  _row_spec(d, False),
                  _mod_spec(1, tps, d, False), _mod_spec(0, tps, d, False), _mod_spec(2, tps, d, False),
                  pl.BlockSpec((ng, gd, gd), lambda i: (0, 0, 0)),
                  _row_spec(d, False), _row_spec(d, False)],
        out_specs=pl.BlockSpec((tm, d), lambda i: (i, 0)),
        scratch_shapes=[pltpu.VMEM((tm + POOL_HALO, d), F32)],
        compiler_params=_params("parallel"),
        name="pool_mixer",
    )(x2, x2, norm_g.reshape(1, d), mod, mod, mod, p["w"].astype(BF16),
      p["b"].reshape(1, d), p["scale"].reshape(1, d))


def _cmp_kernel(x_ref, pe_ref, w1_ref, w2_ref, o_ref, *, n_sub):
    dh = x_ref.shape[1]
    acc_a = jnp.zeros((n_sub, dh), F32)
    acc_b = jnp.zeros((n_sub, dh), F32)
    for p in range(CMP_STRIDE):
        xp = x_ref[pl.ds(p, n_sub, stride=CMP_STRIDE), :]
        acc_a = acc_a + _dot((xp + pe_ref[p:p + 1, :]).astype(BF16), w1_ref[p])
        q = CMP_STRIDE + p
        acc_b = acc_b + _dot((xp + pe_ref[q:q + 1, :]).astype(BF16), w1_ref[q])
    hid = acc_a + pltpu.roll(acc_b, n_sub - 1, 0)
    hid = hid * _sigmoid(hid)
    o_ref[...] = _dot(hid.astype(BF16), w2_ref[...])


def _nsa_compress(kvc3, pe, w1, w2):
    b, s, _ = kvc3.shape
    g, dh = NSA_KV_GROUPS, NSA_HEAD_DIM
    n_sub = s // CMP_STRIDE
    return pl.pallas_call(
        functools.partial(_cmp_kernel, n_sub=n_sub),
        out_shape=jax.ShapeDtypeStruct((2, b, g, n_sub, dh), F32),
        grid=(2, b, g),
        in_specs=[pl.BlockSpec((None, s, dh), lambda kv, bi, gi: (bi, 0, kv * g + gi)),
                  pl.BlockSpec((None, CMP_BLOCK, dh), lambda kv, bi, gi: (kv, 0, 0)),
                  pl.BlockSpec((None, CMP_BLOCK, dh, dh), lambda kv, bi, gi: (kv, 0, 0, 0)),
                  pl.BlockSpec((None, dh, dh), lambda kv, bi, gi: (kv, 0, 0))],
        out_specs=pl.BlockSpec((None, None, None, n_sub, dh), lambda kv, bi, gi: (kv, bi, gi, 0, 0)),
        compiler_params=_params("parallel", "parallel", "parallel"),
        name="nsa_compress",
    )(kvc3, pe, w1.astype(BF16), w2.astype(BF16))


def _sel_block_of(pos):
    assert SEL_BLOCK & (SEL_BLOCK - 1) == 0
    return jnp.right_shift(pos, SEL_BLOCK.bit_length() - 1)


def _split3(x):
    hi = x.astype(BF16)
    r = x - hi.astype(F32)
    mid = r.astype(BF16)
    lo = (r - mid.astype(F32)).astype(BF16)
    return hi, mid, lo


def _nsa_kernel(q_ref, ks_ref, vs_ref, kw_ref, vw_ref, kc_ref, vc_ref, gt_ref, at_ref, et_ref, wb_ref, o_ref,
                *, tq, tk):
    hpg, dh = NSA_HPG, NSA_HEAD_DIM
    assert tq == dh
    ng, n_cmp, _ = kc_ref.shape
    n_sel = at_ref.shape[0]
    rows = hpg * tq
    q0 = pl.program_id(2) * tq
    c = dh ** -0.5 * LOG2E
    tpos = q0 + _iota((tq, 1), 0)
    gcols = lambda g: slice(g * dh, (g + 1) * dh)

    def select(g):
        q = q_ref[:, g * hpg * dh:(g + 1) * hpg * dh]
        q4 = jnp.concatenate([q[:, h * dh:(h + 1) * dh] for h in range(hpg)], axis=0)
        sc = _dot_nt(q4, kc_ref[g].astype(BF16)).reshape(hpg, tq, n_cmp)
        cvalid = (_iota((tq, n_cmp), 1) * CMP_STRIDE + (CMP_BLOCK - 1)) <= tpos
        sc = sc + jnp.where(cvalid, 0.0, NEG)[None]
        e = jnp.exp2((sc - jnp.max(sc, axis=-1, keepdims=True)) * c)
        l = jnp.sum(e, axis=-1, keepdims=True)
        row_ok = jnp.where(tpos >= CMP_BLOCK - 1, 1.0, 0.0)
        p_cmp = e * (row_ok / jnp.maximum(l, 1e-30))
        o_cmp = _dot(p_cmp.reshape(rows, n_cmp).astype(BF16), vc_ref[g].astype(BF16))
        pg = p_cmp[0]
        for h in range(1, hpg):
            pg = pg + p_cmp[h]

        at = at_ref[...]
        imp_t = sum(_dot_nt(at, piece) for piece in _split3(pg))
        blk = _iota((n_sel, tq), 0)
        cur = _sel_block_of(q0 + _iota((n_sel, tq), 1))
        forced = (blk == 0) | (blk == cur) | (blk == cur - 1)
        val = jnp.where(forced, FORCE, jnp.where(blk > cur, NEG, imp_t))
        ranks = []
        for r0 in range(0, n_sel, 8):
            mine = val[r0:r0 + 8, :]
            rank = jnp.zeros((8, tq), F32)
            for mm in range(n_sel):
                other = val[mm:mm + 1, :]
                if mm < r0:
                    rank = rank + jnp.where(other >= mine, 1.0, 0.0)
                elif mm >= r0 + 8:
                    rank = rank + jnp.where(other > mine, 1.0, 0.0)
                else:
                    wins_tie = jnp.where(_iota((8, tq), 0) + r0 > mm, 1.0, 0.0)
                    rank = rank + jnp.where(other > mine, 1.0, jnp.where(other == mine, wins_tie, 0.0))
            ranks.append(rank)
        rank = jnp.concatenate(ranks, axis=0)
        keep = jnp.where(rank < SEL_TOPK, jnp.where(blk <= cur, 0.0, NEG), NEG)
        keep = jnp.concatenate([keep, jnp.zeros((dh - n_sel, tq), F32)], axis=0).T.astype(BF16)
        q_sel = jnp.concatenate([q4, jnp.concatenate([keep] * hpg, axis=0)], axis=1)
        return o_cmp, q4, q_sel

    o_cmp, q4, q_sel = zip(*[select(g) for g in range(ng)])

    def sel_tile(j, carry, last):
        k0 = pl.multiple_of(j * tk, tk)
        onehot = et_ref[pl.ds(k0, tk), :]
        out = []
        for g in range(ng):
            m_i, acc = carry[g]
            k_aug = jnp.concatenate([ks_ref[pl.ds(k0, tk), gcols(g)], onehot], axis=1)
            s = _dot_nt(q_sel[g], k_aug)
            if last:
                causal = (k0 + _iota((tq, tk), 1)) <= tpos
                s = jnp.where(causal[None], s.reshape(hpg, tq, tk), NEG).reshape(rows, tk)
            m_new = jnp.maximum(m_i, jnp.max(s, axis=-1, keepdims=True))
            alpha = jnp.exp2((m_i - m_new) * c)
            e = jnp.exp2((s - m_new) * c).astype(BF16)
            v_aug = jnp.concatenate([vs_ref[pl.ds(k0, tk), gcols(g)], jnp.ones((tk, dh), BF16)], axis=1)
            out.append((m_new, alpha * acc + _dot(e, v_aug)))
        return tuple(out)

    j_last = q0 // tk
    init = tuple((jnp.full((rows, 1), NEG, F32), jnp.zeros((rows, 2 * dh), F32)) for _ in range(ng))
    carry = lax.fori_loop(0, j_last // 2,
                          lambda p, cr: sel_tile(2 * p + 1, sel_tile(2 * p, cr, False), False), init)
    carry = lax.fori_loop(j_last // 2 * 2, j_last, lambda j, cr: sel_tile(j, cr, False), carry)
    carry = sel_tile(j_last, carry, True)

    band = WINDOW + tq
    k0w = pl.multiple_of(jnp.maximum(q0 - WINDOW, 0), tq)
    eye = jnp.where(_iota((rows, dh), 1) == (_iota((rows, dh), 0) & (tq - 1)), 1.0, 0.0).astype(BF16)
    for g in range(ng):
        acc = carry[g][1]
        o_sel = acc[:, :dh] / jnp.maximum(acc[:, dh:], 1e-30)
        q_win = jnp.concatenate([q4[g], eye], axis=1)
        k_aug = jnp.concatenate([kw_ref[pl.ds(k0w, band), gcols(g)], wb_ref[...]], axis=1)
        sw = _dot_nt(q_win, k_aug)
        e = jnp.exp2((sw - jnp.max(sw, axis=-1, keepdims=True)) * c).astype(BF16)
        v_aug = jnp.concatenate([vw_ref[pl.ds(k0w, band), gcols(g)], jnp.ones((band, dh), BF16)], axis=1)
        pv = _dot(e, v_aug)
        o_win = pv[:, :dh] / jnp.maximum(pv[:, dh:], 1e-30)

        gt = gt_ref[g]
        for h in range(hpg):
            g_cmp, g_sel, g_win = (gt[:, 3 * h + r:3 * h + r + 1] for r in range(3))
            hs = slice(h * tq, (h + 1) * tq)
            o_ref[:, (g * hpg + h) * dh:(g * hpg + h + 1) * dh] = (
                g_cmp * o_cmp[g][hs] + g_sel * o_sel[hs] + g_win * o_win[hs]).astype(o_ref.dtype)


def _nsa_attention(proj, kv_cmp, gates, seq):
    b = proj.shape[0]
    g, hpg, dh = NSA_KV_GROUPS, NSA_HPG, NSA_HEAD_DIM
    tq, tk = NSA_Q_TILE, NSA_KEY_TILE
    n_cmp = kv_cmp.shape[3]
    n_sel = seq // SEL_BLOCK
    d_q = g * hpg * dh
    col_q, col_slc, col_win = 0, d_q // dh, (d_q + 2 * g * dh) // dh

    sub = jnp.arange(n_cmp)[None, :] // (SEL_BLOCK // CMP_STRIDE)
    nxt = (jnp.arange(n_cmp)[None, :] + 1) // (SEL_BLOCK // CMP_STRIDE)
    rows = jnp.arange(n_sel)[:, None]
    agg = ((sub == rows).astype(F32) + (nxt == rows).astype(F32)).astype(BF16)
    assert n_sel <= dh
    block_onehot = (jnp.arange(seq)[:, None] // SEL_BLOCK == jnp.arange(dh)[None, :]).astype(BF16)
    band = WINDOW + tq
    n_pat = WINDOW // tq + 1
    pat_q0 = jnp.arange(n_pat)[:, None, None] * tq
    key = jnp.maximum(pat_q0 - WINDOW, 0) + jnp.arange(band)[None, :, None]
    qry = pat_q0 + jnp.arange(tq)[None, None, :]
    win_bias = jnp.where((key <= qry) & (key > qry - WINDOW), 0.0, NEG).astype(BF16)

    ng = NSA_GROUPS_PER_STEP
    assert g % ng == 0 and col_slc % ng == 0 and col_win % ng == 0
    kv_spec = lambda col: pl.BlockSpec((None, seq, ng * dh), lambda bi, gi, i: (bi, 0, col // ng + gi))
    cmp_spec = lambda kv: pl.BlockSpec((None, None, ng, n_cmp, dh), lambda bi, gi, i: (kv, bi, gi, 0, 0))
    return pl.pallas_call(
        functools.partial(_nsa_kernel, tq=tq, tk=tk),
        out_shape=jax.ShapeDtypeStruct((b, seq, d_q), BF16),
        grid=(b, g // ng, seq // tq),
        in_specs=[pl.BlockSpec((None, tq, ng * hpg * dh), lambda bi, gi, i: (bi, i, col_q + gi)),
                  kv_spec(col_slc), kv_spec(col_slc + g), kv_spec(col_win), kv_spec(col_win + g),
                  cmp_spec(0), cmp_spec(1),
                  pl.BlockSpec((None, ng, tq, 3 * hpg), lambda bi, gi, i: (bi, gi, i, 0)),
                  pl.BlockSpec((n_sel, n_cmp), lambda bi, gi, i: (0, 0)),
                  pl.BlockSpec((seq, dh), lambda bi, gi, i: (0, 0)),
                  pl.BlockSpec((None, band, tq), lambda bi, gi, i: (jnp.minimum(i, n_pat - 1), 0, 0))],
        out_specs=pl.BlockSpec((None, tq, ng * hpg * dh), lambda bi, gi, i: (bi, i, gi)),
        compiler_params=_params("parallel", "parallel", "arbitrary"),
        name="nsa_attention",
    )(proj, proj, proj, proj, proj, kv_cmp, kv_cmp, gates, agg, block_onehot, win_bias)


def _nsa_layer(x2, mod, norm_g, p, batch, seq):
    m, d = x2.shape
    g, hpg, dh = NSA_KV_GROUPS, NSA_HPG, NSA_HEAD_DIM
    w_a = jnp.concatenate([p["w_q"], p["w_kv_slc"], p["w_kv_win"]], axis=1).astype(BF16)
    n_gate = p["w_gate"].shape[1]
    gate_pad = 128 - n_gate
    w_b = jnp.concatenate([p["w_kv_cmp"], jnp.pad(p["w_gate"], ((0, 0), (0, gate_pad)))], axis=1).astype(BF16)
    n_cmp_cols = p["w_kv_cmp"].shape[1]
    b_b = jnp.concatenate([jnp.zeros((n_cmp_cols,), F32), jnp.pad(p["b_gate"], (0, gate_pad))])
    proj = _nm_mm(x2, mod, norm_g, w_a, jnp.zeros((w_a.shape[1],), F32), seq, BF16,
                  tm=PROJ_TILE[0], tn=PROJ_TILE[1])
    proj_b = _nm_mm(x2, mod, norm_g, w_b, b_b, seq, F32, tm=ROW_TILE, tn=w_b.shape[1], sig_from=n_cmp_cols)
    proj_b3 = proj_b.reshape(batch, seq, proj_b.shape[1])
    kv_cmp = _nsa_compress(proj_b3, p["cmp_pe"], p["cmp_w1"], p["cmp_w2"])
    gates = proj_b3[:, :, n_cmp_cols:n_cmp_cols + n_gate].reshape(batch, seq, g, 3 * hpg)
    gates = jnp.transpose(gates, (0, 2, 1, 3))
    o = _nsa_attention(proj.reshape(batch, seq, proj.shape[1]), kv_cmp, gates, seq)
    return _mm_res(o.reshape(m, d), p["w_o"].astype(BF16), x2, mod, seq)


def _rope(x, cos2, sin2):
    half = x.shape[1] // 2
    swapped = jnp.concatenate([x[:, half:], x[:, :half]], axis=-1)
    return x * cos2 + swapped * sin2


def _mla_proj_kernel(x_ref, g_ref, sc_ref, sh_ref, w_ref, qg_ref, kvg_ref, cos_ref, sin_ref,
                     wuq_ref, wuk_ref, wuv_ref, q_ref, k_ref, v_ref):
    h = _norm_mod(x_ref[...], g_ref[...], sc_ref[...], sh_ref[...]).astype(BF16)
    y = _dot(h, w_ref[...])
    r0, r1, r2 = MLA_Q_RANK, MLA_Q_RANK + MLA_KV_RANK, MLA_Q_RANK + MLA_KV_RANK + MLA_ROPE_DIM
    cos2, sin2 = cos_ref[...], sin_ref[...]
    cq = _rms(y[:, :r0], qg_ref[...]).astype(BF16)
    ckv = _rms(y[:, r0:r1], kvg_ref[...]).astype(BF16)
    kr = _rope(y[:, r1:r2], cos2, sin2).astype(BF16)
    for hd in range(q_ref.shape[0]):
        yq = _dot(cq, wuq_ref[hd])
        qr = _rope(yq[:, MLA_NOPE_DIM:], cos2, sin2)
        q_ref[hd] = jnp.concatenate([yq[:, :MLA_NOPE_DIM], qr], axis=-1).astype(BF16)
        k_ref[hd] = jnp.concatenate([_dot(ckv, wuk_ref[hd]).astype(BF16), kr], axis=-1)
        v_ref[hd] = _dot(ckv, wuv_ref[hd]).astype(BF16)


def _mla_attn_kernel(q_ref, k_ref, v_ref, o_ref, *, tq, tk):
    dqk = q_ref.shape[2]
    nh, _, dv = v_ref.shape
    assert tq == tk
    c = dqk ** -0.5 * LOG2E
    ones = jnp.ones((tk, dv), BF16)

    def tile(j, carry, diagonal):
        k0 = pl.multiple_of(j * tk, tk)
        out = []
        for h in range(nh):
            m_i, acc = carry[h]
            s = _dot_nt(q_ref[h], k_ref[h, pl.ds(k0, tk), :])
            if diagonal:
                s = jnp.where(_iota((tq, tk), 1) <= _iota((tq, tk), 0), s, NEG)
            m_new = jnp.maximum(m_i, jnp.max(s, axis=-1, keepdims=True))
            alpha = jnp.exp2((m_i - m_new) * c)
            e = jnp.exp2((s - m_new) * c).astype(BF16)
            v_aug = jnp.concatenate([v_ref[h, pl.ds(k0, tk), :], ones], axis=1)
            out.append((m_new, alpha * acc + _dot(e, v_aug)))
        return tuple(out)

    i = pl.program_id(2)
    init = tuple((jnp.full((tq, 1), NEG, F32), jnp.zeros((tq, 2 * dv), F32)) for _ in range(nh))
    carry = lax.fori_loop(0, i // 2, lambda p, cr: tile(2 * p + 1, tile(2 * p, cr, False), False), init)
    carry = lax.fori_loop(i // 2 * 2, i, lambda j, cr: tile(j, cr, False), carry)
    carry = tile(i, carry, True)
    for h in range(nh):
        acc = carry[h][1]
        o_ref[:, h * dv:(h + 1) * dv] = (acc[:, :dv] / jnp.maximum(acc[:, dv:], 1e-30)).astype(o_ref.dtype)


def _mla_layer(x2, mod, norm_g, p, batch, seq):
    m, d = x2.shape
    hh, dn, dr, dv = MLA_HEADS, MLA_NOPE_DIM, MLA_ROPE_DIM, MLA_V_DIM
    rq, rkv = MLA_Q_RANK, MLA_KV_RANK
    tm = ROW_TILE
    tps = seq // tm

    pos = jnp.arange(seq, dtype=F32)
    inv_freq = ROPE_THETA ** (-jnp.arange(0, dr, 2, dtype=F32) / dr)
    ang = pos[:, None] * inv_freq[None, :]
    cos, sin = jnp.cos(ang), jnp.sin(ang)
    cos2 = jnp.concatenate([cos, cos], axis=-1)
    sin2 = jnp.concatenate([-sin, sin], axis=-1)
    rope_spec = pl.BlockSpec((tm, dr), lambda i, *_: (i % tps, 0))

    n_down = rq + rkv + dr
    n_pad = -n_down % 128
    w_down = jnp.pad(jnp.concatenate([p["w_dq"], p["w_dkv"]], axis=1), ((0, 0), (0, n_pad))).astype(BF16)
    dqk = dn + dr
    w_uq = jnp.transpose(p["w_uq"].reshape(rq, hh, dqk), (1, 0, 2)).astype(BF16)
    w_uk = jnp.transpose(p["w_uk"].reshape(rkv, hh, dn), (1, 0, 2)).astype(BF16)
    w_uv = jnp.transpose(p["w_uv"].reshape(rkv, hh, dv), (1, 0, 2)).astype(BF16)
    whole = lambda a: pl.BlockSpec(a.shape, lambda i: (0,) * a.ndim)
    head_out = lambda width: pl.BlockSpec((None, hh, tm, width), lambda i: (i // tps, 0, i % tps, 0))
    qf, kf, vf = pl.pallas_call(
        _mla_proj_kernel,
        out_shape=(jax.ShapeDtypeStruct((batch, hh, seq, dqk), BF16),
                   jax.ShapeDtypeStruct((batch, hh, seq, dqk), BF16),
                   jax.ShapeDtypeStruct((batch, hh, seq, dv), BF16)),
        grid=(m // tm,),
        in_specs=[pl.BlockSpec((tm, d), lambda i: (i, 0)),
                  _row_spec(d, False),
                  _mod_spec(1, tps, d, False), _mod_spec(0, tps, d, False),
                  whole(w_down), _row_spec(rq, False), _row_spec(rkv, False), rope_spec, rope_spec,
                  whole(w_uq), whole(w_uk), whole(w_uv)],
        out_specs=(head_out(dqk), head_out(dqk), head_out(dv)),
        compiler_params=_params("parallel"),
        name="mla_proj",
    )(x2, norm_g.reshape(1, d), mod, mod, w_down, p["q_norm_g"].reshape(1, rq),
      p["kv_norm_g"].reshape(1, rkv), cos2, sin2, w_uq, w_uk, w_uv)

    tq = tk = MLA_SEQ_TILE
    nh = MLA_HEADS_PER_STEP
    o = pl.pallas_call(
        functools.partial(_mla_attn_kernel, tq=tq, tk=tk),
        out_shape=jax.ShapeDtypeStruct((batch, seq, hh * dv), BF16),
        grid=(batch, hh // nh, seq // tq),
        in_specs=[pl.BlockSpec((None, nh, tq, dqk), lambda b, h, i: (b, h, i, 0)),
                  pl.BlockSpec((None, nh, seq, dqk), lambda b, h, i: (b, h, 0, 0)),
                  pl.BlockSpec((None, nh, seq, dv), lambda b, h, i: (b, h, 0, 0))],
        out_specs=pl.BlockSpec((None, tq, nh * dv), lambda b, h, i: (b, i, h)),
        compiler_params=_params("parallel", "parallel", "arbitrary"),
        name="mla_attention",
    )(qf, kf, vf)
    return _mm_res(o.reshape(m, hh * dv), p["w_o"].astype(BF16), x2, mod, seq)


def kernel(x, c, ada_w, ada_b, norm1_g, norm2_g, mlp_w1, mlp_w2, final_g, conv_w_in, conv_b_in, conv_w_dw, conv_b_dw, conv_ln_g, conv_ln_b, conv_w_out, conv_b_out, nsa_w_q, nsa_w_kv_cmp, nsa_w_kv_slc, nsa_w_kv_win, nsa_cmp_pe, nsa_cmp_w1, nsa_cmp_w2, nsa_w_gate, nsa_b_gate, nsa_w_o, pool_w, pool_b, pool_scale, mla_w_dq, mla_q_norm_g, mla_w_uq, mla_w_dkv, mla_kv_norm_g, mla_w_uk, mla_w_uv, mla_w_o):
    batch, seq, d = x.shape
    depth = ada_w.shape[0]
    n_mixers = 4
    mods = _ada_mod(c, ada_w, ada_b)
    x2 = x.reshape(batch * seq, d)
    w1_all, w2_all = mlp_w1.astype(BF16), mlp_w2.astype(BF16)
    for i in range(depth):
        kind, u = i % n_mixers, i // n_mixers
        mod = mods[i].reshape(batch * 6, 1, d)
        if kind == 0:
            p = dict(w_in=conv_w_in[u], b_in=conv_b_in[u], w_dw=conv_w_dw[u], b_dw=conv_b_dw[u],
                     ln_g=conv_ln_g[u], ln_b=conv_ln_b[u], w_out=conv_w_out[u], b_out=conv_b_out[u])
            x2 = _conv_layer(x2, mod, norm1_g[i], p, seq)
        elif kind == 1:
            p = dict(w_q=nsa_w_q[u], w_kv_cmp=nsa_w_kv_cmp[u], w_kv_slc=nsa_w_kv_slc[u],
                     w_kv_win=nsa_w_kv_win[u], cmp_pe=nsa_cmp_pe[u], cmp_w1=nsa_cmp_w1[u],
                     cmp_w2=nsa_cmp_w2[u], w_gate=nsa_w_gate[u], b_gate=nsa_b_gate[u], w_o=nsa_w_o[u])
            x2 = _nsa_layer(x2, mod, norm1_g[i], p, batch, seq)
        elif kind == 2:
            p = dict(w=pool_w[u], b=pool_b[u], scale=pool_scale[u])
            x2 = _pool_layer(x2, mod, norm1_g[i], p, seq)
        else:
            p = dict(w_dq=mla_w_dq[u], q_norm_g=mla_q_norm_g[u], w_uq=mla_w_uq[u], w_dkv=mla_w_dkv[u],
                     kv_norm_g=mla_kv_norm_g[u], w_uk=mla_w_uk[u], w_uv=mla_w_uv[u], w_o=mla_w_o[u])
            x2 = _mla_layer(x2, mod, norm1_g[i], p, batch, seq)
        x2 = _mlp(x2, mod, norm2_g[i], w1_all, w2_all, i, seq, final_g=final_g if i == depth - 1 else None)
    return x2.reshape(batch, seq, d)
```

```python
import functools

import jax
import jax.numpy as jnp
from jax import lax
from jax.experimental import pallas as pl
from jax.experimental.pallas import tpu as pltpu

F32 = jnp.float32
BF16 = jnp.bfloat16

EPS = 1e-6
NEG = -1e30
FORCE = 1e30
LOG2E = 1.4426950408889634

CONV_WIDTH = 31
CONV_HALO = 32
NSA_HEADS = 16
NSA_HEAD_DIM = 128
NSA_KV_GROUPS = 4
NSA_HPG = NSA_HEADS // NSA_KV_GROUPS
NSA_GROUPS_PER_STEP = 4
CMP_BLOCK = 32
CMP_STRIDE = 16
SEL_BLOCK = 64
SEL_TOPK = 16
WINDOW = 512
POOL_WINDOWS = (2, 4, 8, 16)
POOL_HALO = 16
MLA_HEADS = 16
MLA_NOPE_DIM = 128
MLA_ROPE_DIM = 64
MLA_V_DIM = 128
MLA_Q_RANK = 512
MLA_KV_RANK = 256
ROPE_THETA = 10000.0

VMEM_LIMIT_BYTES = 56 * 1024 * 1024
LANES = 128
SUBLANES = 8

ROW_TILE = 512
MLP_FF_TILE = 1024
ADA_COL_TILE = 1024
PROJ_TILE = (1024, 1024)
GLU_TILE = (1024, 512)
CONV_TILE = (512, 1024)
POOL_ROW_TILE = 256
NSA_Q_TILE, NSA_KEY_TILE = 128, 512
MLA_SEQ_TILE = 512
MLA_HEADS_PER_STEP = 4


def _params(*semantics):
    return pltpu.CompilerParams(dimension_semantics=semantics, vmem_limit_bytes=VMEM_LIMIT_BYTES)


def _dot(a, b):
    return jnp.dot(a, b, preferred_element_type=F32)


def _dot_nt(a, b):
    return lax.dot_general(a, b, (((1,), (1,)), ((), ())), preferred_element_type=F32)


def _rms(x, g):
    return x * lax.rsqrt(jnp.mean(x * x, axis=-1, keepdims=True) + EPS) * g


def _norm_mod(x, g, sc, sh):
    return _rms(x, g) * (1.0 + sc) + sh


def _sigmoid(x):
    return 1.0 / (1.0 + jnp.exp(-x))


def _iota(shape, dim):
    return lax.broadcasted_iota(jnp.int32, shape, dim)


def _mod_spec(k, tiles_per_seq, width, tiled):
    if tiled:
        return pl.BlockSpec((None, 1, width), lambda i, j: ((i // tiles_per_seq) * 6 + k, 0, j))
    return pl.BlockSpec((None, 1, width), lambda i, *_: ((i // tiles_per_seq) * 6 + k, 0, 0))


def _row_spec(width, tiled):
    if tiled:
        return pl.BlockSpec((1, width), lambda i, j: (0, j))
    return pl.BlockSpec((1, width), lambda i, *_: (0, 0))


def _ada_kernel(c_ref, w_ref, b_ref, o_ref):
    c = c_ref[...]
    cs = c * _sigmoid(c)
    o_ref[...] = _dot(cs.astype(BF16), w_ref[...].astype(BF16)) + b_ref[...]


def _ada_mod(c, ada_w, ada_b):
    depth, d, n = ada_w.shape
    b = c.shape[0]
    bp = SUBLANES
    tn = ADA_COL_TILE
    cp = jnp.pad(c, ((0, bp - b), (0, 0)))
    out = pl.pallas_call(
        _ada_kernel,
        out_shape=jax.ShapeDtypeStruct((depth, bp, n), F32),
        grid=(depth, n // tn),
        in_specs=[pl.BlockSpec((bp, d), lambda l, j: (0, 0)),
                  pl.BlockSpec((None, d, tn), lambda l, j: (l, 0, j)),
                  pl.BlockSpec((None, 1, tn), lambda l, j: (l, 0, j))],
        out_specs=pl.BlockSpec((None, bp, tn), lambda l, j: (l, 0, j)),
        compiler_params=_params("parallel", "parallel"),
        name="ada_mod",
    )(cp, ada_w, ada_b.reshape(depth, 1, n))
    return out[:, :b]


def _with_next_tile_norm(x0_ref, xn_ref, g_ref, sc_ref, sh_ref, scn_ref, shn_ref, h_even, h_odd, body):
    i, j = pl.program_id(0), pl.program_id(1)
    rows = xn_ref.shape[0]

    @pl.when((i == 0) & (j == 0))
    def _():
        h_even[...] = _norm_mod(x0_ref[...], g_ref[...], sc_ref[...], sh_ref[...]).astype(BF16)

    def step(h_cur, h_next):
        body(h_cur)
        r0 = pl.multiple_of(j * rows, rows)
        h_next[pl.ds(r0, rows), :] = _norm_mod(
            xn_ref[...], g_ref[...], scn_ref[...], shn_ref[...]).astype(BF16)

    @pl.when(i % 2 == 0)
    def _():
        step(h_even, h_odd)

    @pl.when(i % 2 == 1)
    def _():
        step(h_odd, h_even)


def _next_tile_specs(m, tm, steps, d, tiles_per_seq, k_scale, k_shift):
    n_tiles = m // tm
    rows = tm // steps
    assert rows * steps == tm and rows % 16 == 0
    nxt = lambda i: jnp.minimum(i + 1, n_tiles - 1)
    mod_next = lambda k: pl.BlockSpec((None, 1, d), lambda i, j: ((nxt(i) // tiles_per_seq) * 6 + k, 0, 0))
    return (pl.BlockSpec((rows, d), lambda i, j: (nxt(i) * steps + j, 0)), mod_next(k_scale), mod_next(k_shift))


def _mlp_kernel(x_ref, xn_ref, g_ref, sc_ref, sh_ref, scn_ref, shn_ref, gate_ref, w1_ref, w2_ref, *rest, final):
    if final:
        fg_ref, o_ref, h_even, h_odd, acc_ref = rest
    else:
        w1f_ref, w2f_ref, o_ref, w1n_ref, w2n_ref, h_even, h_odd, acc_ref = rest
    f = pl.program_id(1)

    @pl.when(f == 0)
    def _():
        acc_ref[...] = jnp.zeros_like(acc_ref)

    def body(h_cur):
        a = _dot(h_cur[...], w1_ref[...])
        a = jnp.square(jnp.maximum(a, 0.0)).astype(BF16)
        acc_ref[...] += _dot(a, w2_ref[...])
        if not final:
            w1n_ref[...] = w1f_ref[...].astype(BF16)
            w2n_ref[...] = w2f_ref[...].astype(BF16)

    _with_next_tile_norm(x_ref, xn_ref, g_ref, sc_ref, sh_ref, scn_ref, shn_ref, h_even, h_odd, body)

    @pl.when(f == pl.num_programs(1) - 1)
    def _():
        out = x_ref[...] + gate_ref[...] * acc_ref[...]
        if final:
            out = _rms(out, fg_ref[...])
        o_ref[...] = out


def _mlp(x2, mod, norm_g, w1, w2, seq, next_f32=None, final_g=None):
    m, d = x2.shape
    dff = w1.shape[1]
    tm, tf = ROW_TILE, MLP_FF_TILE
    tps = seq // tm
    final = final_g is not None
    nt, nf = m // tm, dff // tf
    xn_spec, scn_spec, shn_spec = _next_tile_specs(m, tm, nf, d, tps, 4, 3)
    in_specs = [pl.BlockSpec((tm, d), lambda i, f: (i, 0)),
                xn_spec,
                _row_spec(d, False),
                _mod_spec(4, tps, d, False), _mod_spec(3, tps, d, False), scn_spec, shn_spec,
                _mod_spec(5, tps, d, False),
                pl.BlockSpec((d, tf), lambda i, f: (0, f)),
                pl.BlockSpec((tf, d), lambda i, f: (f, 0))]
    args = [x2, x2, norm_g.reshape(1, d), mod, mod, mod, mod, mod, w1, w2]
    out_shape = [jax.ShapeDtypeStruct((m, d), F32)]
    out_specs = [pl.BlockSpec((tm, d), lambda i, f: (i, 0))]
    if final:
        in_specs.append(_row_spec(d, False))
        args.append(final_g.reshape(1, d))
    else:
        w1_all, w2_all, nxt = next_f32
        steps = nt * nf
        r1, c1 = 2 * d // steps, dff // 2
        r2 = dff // steps
        assert r1 * (steps // 2) == d and r2 * steps == dff and r1 % 16 == 0 and r2 % 16 == 0
        step = lambda i, f: i * nf + f
        in_specs += [pl.BlockSpec((None, r1, c1), lambda i, f: (nxt, step(i, f) // 2, step(i, f) % 2)),
                     pl.BlockSpec((None, r2, d), lambda i, f: (nxt, step(i, f), 0))]
        args += [w1_all, w2_all]
        out_shape += [jax.ShapeDtypeStruct((d, dff), BF16), jax.ShapeDtypeStruct((dff, d), BF16)]
        out_specs += [pl.BlockSpec((r1, c1), lambda i, f: (step(i, f) // 2, step(i, f) % 2)),
                      pl.BlockSpec((r2, d), lambda i, f: (step(i, f), 0))]
    out = pl.pallas_call(
        functools.partial(_mlp_kernel, final=final),
        out_shape=tuple(out_shape),
        grid=(nt, nf),
        in_specs=in_specs,
        out_specs=tuple(out_specs),
        scratch_shapes=[pltpu.VMEM((tm, d), BF16), pltpu.VMEM((tm, d), BF16), pltpu.VMEM((tm, d), F32)],
        compiler_params=_params("arbitrary", "arbitrary"),
        name="mlp",
    )(*args)
    return out[0] if final else out


def _nm_mm_kernel(x0_ref, xn_ref, g_ref, sc_ref, sh_ref, scn_ref, shn_ref, w_ref, b_ref, o_ref, h_even, h_odd,
                  *, sig_from, tn):
    def body(h_cur):
        y = _dot(h_cur[...], w_ref[...]) + b_ref[...]
        if sig_from is not None:
            col = pl.program_id(1) * tn + _iota(y.shape, 1)
            y = jnp.where(col >= sig_from, _sigmoid(y), y)
        o_ref[...] = y.astype(o_ref.dtype)

    _with_next_tile_norm(x0_ref, xn_ref, g_ref, sc_ref, sh_ref, scn_ref, shn_ref, h_even, h_odd, body)


def _first_tile_specs(tm, d):
    first_mod = lambda k: pl.BlockSpec((None, 1, d), lambda i, j: (k, 0, 0))
    return pl.BlockSpec((tm, d), lambda i, j: (0, 0)), first_mod(1), first_mod(0)


def _nm_mm(x2, mod, norm_g, w, bias, seq, out_dtype, tm, tn, sig_from=None):
    m, d = x2.shape
    n = w.shape[1]
    tps = seq // tm
    x0_spec, sc0_spec, sh0_spec = _first_tile_specs(tm, d)
    xn_spec, scn_spec, shn_spec = _next_tile_specs(m, tm, n // tn, d, tps, 1, 0)
    return pl.pallas_call(
        functools.partial(_nm_mm_kernel, sig_from=sig_from, tn=tn),
        out_shape=jax.ShapeDtypeStruct((m, n), out_dtype),
        grid=(m // tm, n // tn),
        in_specs=[x0_spec, xn_spec, _row_spec(d, False), sc0_spec, sh0_spec, scn_spec, shn_spec,
                  pl.BlockSpec((d, tn), lambda i, j: (0, j)),
                  _row_spec(tn, True)],
        out_specs=pl.BlockSpec((tm, tn), lambda i, j: (i, j)),
        scratch_shapes=[pltpu.VMEM((tm, d), BF16), pltpu.VMEM((tm, d), BF16)],
        compiler_params=_params("arbitrary", "arbitrary"),
        name="norm_mod_matmul",
    )(x2, x2, norm_g.reshape(1, d), mod, mod, mod, mod, w, bias.reshape(1, n))


def _mm_res_kernel(a_ref, w_ref, x_ref, gate_ref, o_ref):
    o_ref[...] = x_ref[...] + gate_ref[...] * _dot(a_ref[...], w_ref[...])


def _mm_res(a, w, x2, mod, seq):
    m, k = a.shape
    n = w.shape[1]
    tm = ROW_TILE
    tps = seq // tm
    return pl.pallas_call(
        _mm_res_kernel,
        out_shape=jax.ShapeDtypeStruct((m, n), F32),
        grid=(m // tm,),
        in_specs=[pl.BlockSpec((tm, k), lambda i: (i, 0)),
                  pl.BlockSpec((k, n), lambda i: (0, 0)),
                  pl.BlockSpec((tm, n), lambda i: (i, 0)),
                  _mod_spec(2, tps, n, False)],
        out_specs=pl.BlockSpec((tm, n), lambda i: (i, 0)),
        compiler_params=_params("parallel"),
        name="matmul_residual",
    )(a, w, x2, mod)


def _glu_kernel(x0_ref, xn_ref, g_ref, sc_ref, sh_ref, scn_ref, shn_ref, wa_ref, wg_ref, ba_ref, bg_ref,
                o_ref, h_even, h_odd):
    def body(h_cur):
        h = h_cur[...]
        a = _dot(h, wa_ref[...]) + ba_ref[...]
        gt = _dot(h, wg_ref[...]) + bg_ref[...]
        o_ref[...] = a * _sigmoid(gt)

    _with_next_tile_norm(x0_ref, xn_ref, g_ref, sc_ref, sh_ref, scn_ref, shn_ref, h_even, h_odd, body)


def _conv_kernel(u_ref, uh_ref, wdw_ref, bdw_ref, lng_ref, lnb_ref, wo_ref, bo_ref, x_ref, gate_ref,
                 o_ref, ubuf, shifted, conv_ref, v_ref, *, tm, tiles_per_seq):
    i = pl.program_id(0)
    d = u_ref.shape[1]
    rc = 64
    rn = 16
    span = tm + CONV_HALO - 8

    @pl.when(pl.program_id(1) == 0)
    def _():
        first = (i % tiles_per_seq) == 0
        ubuf[0:CONV_HALO, :] = jnp.where(first, 0.0, uh_ref[...])
        ubuf[CONV_HALO:, :] = u_ref[...]

        def column(ct, carry):
            cs = pl.ds(pl.multiple_of(ct * LANES, LANES), LANES)
            for r in range(1, 8):
                shifted[r - 1] = ubuf[r:r + span, cs]
            taps = [jnp.broadcast_to(wdw_ref[k:k + 1, cs], (8, LANES)) for k in range(CONV_WIDTH)]
            bias = jnp.broadcast_to(bdw_ref[:, cs], (8, LANES))

            def chunk(c, inner):
                r0 = pl.multiple_of(c * rc, rc)
                for v in range(rc // 8):
                    acc = bias
                    for k in range(CONV_WIDTH):
                        off = CONV_HALO - (CONV_WIDTH - 1) + k
                        q, r = divmod(off, 8)
                        start = pl.multiple_of(r0 + 8 * (q + v), 8)
                        rows = ubuf[pl.ds(start, 8), cs] if r == 0 else shifted[r - 1, pl.ds(start, 8), :]
                        acc = acc + taps[k] * rows
                    conv_ref[pl.ds(pl.multiple_of(r0 + 8 * v, 8), 8), cs] = acc
                return inner

            lax.fori_loop(0, tm // rc, chunk, 0)
            return carry

        lax.fori_loop(0, d // LANES, column, 0)

        def norm_chunk(c, carry):
            r0 = pl.multiple_of(c * rn, rn)
            acc = conv_ref[pl.ds(r0, rn), :]
            mu = jnp.mean(acc, axis=-1, keepdims=True)
            cen = acc - mu
            var = jnp.mean(cen * cen, axis=-1, keepdims=True)
            y = cen * lax.rsqrt(var + EPS) * lng_ref[...] + lnb_ref[...]
            v_ref[pl.ds(r0, rn), :] = (y * _sigmoid(y)).astype(BF16)
            return carry

        lax.fori_loop(0, tm // rn, norm_chunk, 0, unroll=4)

    y = _dot(v_ref[...], wo_ref[...]) + bo_ref[...]
    o_ref[...] = x_ref[...] + gate_ref[...] * y


def _conv_layer(x2, mod, norm_g, p, seq):
    m, d = x2.shape
    tm, tn = GLU_TILE
    tps = seq // tm
    w_in = p["w_in"].astype(BF16)
    x0_spec, sc0_spec, sh0_spec = _first_tile_specs(tm, d)
    xn_spec, scn_spec, shn_spec = _next_tile_specs(m, tm, d // tn, d, tps, 1, 0)
    u = pl.pallas_call(
        _glu_kernel,
        out_shape=jax.ShapeDtypeStruct((m, d), F32),
        grid=(m // tm, d // tn),
        in_specs=[x0_spec, xn_spec, _row_spec(d, False), sc0_spec, sh0_spec, scn_spec, shn_spec,
                  pl.BlockSpec((d, tn), lambda i, j: (0, j)),
                  pl.BlockSpec((d, tn), lambda i, j: (0, j + d // tn)),
                  pl.BlockSpec((1, tn), lambda i, j: (0, j)),
                  pl.BlockSpec((1, tn), lambda i, j: (0, j + d // tn))],
        out_specs=pl.BlockSpec((tm, tn), lambda i, j: (i, j)),
        scratch_shapes=[pltpu.VMEM((tm, d), BF16), pltpu.VMEM((tm, d), BF16)],
        compiler_params=_params("arbitrary", "arbitrary"),
        name="conv_glu",
    )(x2, x2, norm_g.reshape(1, d), mod, mod, mod, mod, w_in, w_in,
      p["b_in"].reshape(1, 2 * d), p["b_in"].reshape(1, 2 * d))

    tm, tn = CONV_TILE
    tps = seq // tm
    hb = tm // CONV_HALO
    return pl.pallas_call(
        functools.partial(_conv_kernel, tm=tm, tiles_per_seq=tps),
        out_shape=jax.ShapeDtypeStruct((m, d), F32),
        grid=(m // tm, d // tn),
        in_specs=[pl.BlockSpec((tm, d), lambda i, j: (i, 0)),
                  pl.BlockSpec((CONV_HALO, d), lambda i, j: (jnp.maximum(i * hb - 1, 0), 0)),
                  pl.BlockSpec((CONV_WIDTH, d), lambda i, j: (0, 0)),
                  _row_spec(d, False), _row_spec(d, False), _row_spec(d, False),
                  pl.BlockSpec((d, tn), lambda i, j: (0, j)),
                  _row_spec(tn, True),
                  pl.BlockSpec((tm, tn), lambda i, j: (i, j)),
                  _mod_spec(2, tps, tn, True)],
        out_specs=pl.BlockSpec((tm, tn), lambda i, j: (i, j)),
        scratch_shapes=[pltpu.VMEM((tm + CONV_HALO, d), F32),
                        pltpu.VMEM((7, tm + CONV_HALO - 8, LANES), F32),
                        pltpu.VMEM((tm, d), F32),
                        pltpu.VMEM((tm, d), BF16)],
        compiler_params=_params("parallel", "arbitrary"),
        name="conv_ln_out",
    )(u, u, p["w_dw"], p["b_dw"].reshape(1, d), p["ln_g"].reshape(1, d), p["ln_b"].reshape(1, d),
      p["w_out"].astype(BF16), p["b_out"].reshape(1, d), x2, mod)


def _pool_kernel(x_ref, xh_ref, g_ref, sc_ref, sh_ref, gate_ref, w_ref, b_ref, scale_ref, o_ref, hbuf,
                 *, tm, tiles_per_seq):
    i = pl.program_id(0)
    d = x_ref.shape[1]
    gd = d // len(POOL_WINDOWS)
    first = (i % tiles_per_seq) == 0
    g, sc, sh = g_ref[...], sc_ref[...], sh_ref[...]
    hbuf[0:POOL_HALO, :] = jnp.where(first, 0.0, _norm_mod(xh_ref[...], g, sc, sh))
    hbuf[POOL_HALO:, :] = _norm_mod(x_ref[...], g, sc, sh)
    tpos = (i % tiles_per_seq) * tm + _iota((tm, 1), 0)
    for gi, win in enumerate(POOL_WINDOWS):
        lo, hi = gi * gd, (gi + 1) * gd
        h = hbuf[POOL_HALO:, lo:hi]
        wsum = h
        for k in range(1, win):
            wsum = wsum + hbuf[POOL_HALO - k:POOL_HALO - k + tm, lo:hi]
        cnt = jnp.minimum(tpos + 1, win).astype(F32)
        dlt = wsum / cnt - h
        y = (_dot(dlt.astype(BF16), w_ref[gi]) + b_ref[:, lo:hi]) * scale_ref[:, lo:hi]
        o_ref[:, lo:hi] = x_ref[:, lo:hi] + gate_ref[:, lo:hi] * y


def _pool_layer(x2, mod, norm_g, p, seq):
    m, d = x2.shape
    tm = POOL_ROW_TILE
    tps = seq // tm
    hb = tm // POOL_HALO
    ng, gd = p["w"].shape[0], p["w"].shape[1]
    return pl.pallas_call(
        functools.partial(_pool_kernel, tm=tm, tiles_per_seq=tps),
        out_shape=jax.ShapeDtypeStruct((m, d), F32),
        grid=(m // tm,),
        in_specs=[pl.BlockSpec((tm, d), lambda i: (i, 0)),
                  pl.BlockSpec((POOL_HALO, d), lambda i: (jnp.maximum(i * hb - 1, 0), 0)),
                  _row_spec(d, False),
                  _mod_spec(1, tps, d, False), _mod_spec(0, tps, d, False), _mod_spec(2, tps, d, False),
                  pl.BlockSpec((ng, gd, gd), lambda i: (0, 0, 0)),
                  _row_spec(d, False), _row_spec(d, False)],
        out_specs=pl.BlockSpec((tm, d), lambda i: (i, 0)),
        scratch_shapes=[pltpu.VMEM((tm + POOL_HALO, d), F32)],
        compiler_params=_params("parallel"),
        name="pool_mixer",
    )(x2, x2, norm_g.reshape(1, d), mod, mod, mod, p["w"].astype(BF16),
      p["b"].reshape(1, d), p["scale"].reshape(1, d))


def _cmp_kernel(x_ref, pe_ref, w1_ref, w2_ref, o_ref, *, n_sub):
    dh = x_ref.shape[1]
    acc_a = jnp.zeros((n_sub, dh), F32)
    acc_b = jnp.zeros((n_sub, dh), F32)
    for p in range(CMP_STRIDE):
        xp = x_ref[pl.ds(p, n_sub, stride=CMP_STRIDE), :]
        acc_a = acc_a + _dot((xp + pe_ref[p:p + 1, :]).astype(BF16), w1_ref[p])
        q = CMP_STRIDE + p
        acc_b = acc_b + _dot((xp + pe_ref[q:q + 1, :]).astype(BF16), w1_ref[q])
    hid = acc_a + pltpu.roll(acc_b, n_sub - 1, 0)
    hid = hid * _sigmoid(hid)
    o_ref[...] = _dot(hid.astype(BF16), w2_ref[...])


def _nsa_compress(kvc3, pe, w1, w2):
    b, s, _ = kvc3.shape
    g, dh = NSA_KV_GROUPS, NSA_HEAD_DIM
    n_sub = s // CMP_STRIDE
    return pl.pallas_call(
        functools.partial(_cmp_kernel, n_sub=n_sub),
        out_shape=jax.ShapeDtypeStruct((2, b, g, n_sub, dh), F32),
        grid=(2, b, g),
        in_specs=[pl.BlockSpec((None, s, dh), lambda kv, bi, gi: (bi, 0, kv * g + gi)),
                  pl.BlockSpec((None, CMP_BLOCK, dh), lambda kv, bi, gi: (kv, 0, 0)),
                  pl.BlockSpec((None, CMP_BLOCK, dh, dh), lambda kv, bi, gi: (kv, 0, 0, 0)),
                  pl.BlockSpec((None, dh, dh), lambda kv, bi, gi: (kv, 0, 0))],
        out_specs=pl.BlockSpec((None, None, None, n_sub, dh), lambda kv, bi, gi: (kv, bi, gi, 0, 0)),
        compiler_params=_params("parallel", "parallel", "parallel"),
        name="nsa_compress",
    )(kvc3, pe, w1.astype(BF16), w2.astype(BF16))


def _sel_block_of(pos):
    assert SEL_BLOCK & (SEL_BLOCK - 1) == 0
    return jnp.right_shift(pos, SEL_BLOCK.bit_length() - 1)


def _split3(x):
    hi = x.astype(BF16)
    r = x - hi.astype(F32)
    mid = r.astype(BF16)
    lo = (r - mid.astype(F32)).astype(BF16)
    return hi, mid, lo


def _nsa_kernel(q_ref, ks_ref, vs_ref, kw_ref, vw_ref, kc_ref, vc_ref, gt_ref, at_ref, et_ref, wb_ref, o_ref,
                *, tq, tk):
    hpg, dh = NSA_HPG, NSA_HEAD_DIM
    assert tq == dh
    ng, n_cmp, _ = kc_ref.shape
    n_sel = at_ref.shape[0]
    rows = hpg * tq
    q0 = pl.program_id(2) * tq
    c = dh ** -0.5 * LOG2E
    tpos = q0 + _iota((tq, 1), 0)
    gcols = lambda g: slice(g * dh, (g + 1) * dh)

    def select(g):
        q = q_ref[:, g * hpg * dh:(g + 1) * hpg * dh]
        q4 = jnp.concatenate([q[:, h * dh:(h + 1) * dh] for h in range(hpg)], axis=0)
        sc = _dot_nt(q4, kc_ref[g].astype(BF16)).reshape(hpg, tq, n_cmp)
        cvalid = (_iota((tq, n_cmp), 1) * CMP_STRIDE + (CMP_BLOCK - 1)) <= tpos
        sc = sc + jnp.where(cvalid, 0.0, NEG)[None]
        e = jnp.exp2((sc - jnp.max(sc, axis=-1, keepdims=True)) * c)
        l = jnp.sum(e, axis=-1, keepdims=True)
        row_ok = jnp.where(tpos >= CMP_BLOCK - 1, 1.0, 0.0)
        p_cmp = e * (row_ok / jnp.maximum(l, 1e-30))
        o_cmp = _dot(p_cmp.reshape(rows, n_cmp).astype(BF16), vc_ref[g].astype(BF16))
        pg = p_cmp[0]
        for h in range(1, hpg):
            pg = pg + p_cmp[h]

        at = at_ref[...]
        imp_t = sum(_dot_nt(at, piece) for piece in _split3(pg))
        blk = _iota((n_sel, tq), 0)
        cur = _sel_block_of(q0 + _iota((n_sel, tq), 1))
        forced = (blk == 0) | (blk == cur) | (blk == cur - 1)
        val = jnp.where(forced, FORCE, jnp.where(blk > cur, NEG, imp_t))
        ranks = []
        for r0 in range(0, n_sel, 8):
            mine = val[r0:r0 + 8, :]
            rank = jnp.zeros((8, tq), F32)
            for mm in range(n_sel):
                other = val[mm:mm + 1, :]
                if mm < r0:
                    rank = rank + jnp.where(other >= mine, 1.0, 0.0)
                elif mm >= r0 + 8:
                    rank = rank + jnp.where(other > mine, 1.0, 0.0)
                else:
                    wins_tie = jnp.where(_iota((8, tq), 0) + r0 > mm, 1.0, 0.0)
                    rank = rank + jnp.where(other > mine, 1.0, jnp.where(other == mine, wins_tie, 0.0))
            ranks.append(rank)
        rank = jnp.concatenate(ranks, axis=0)
        keep = jnp.where(rank < SEL_TOPK, jnp.where(blk <= cur, 0.0, NEG), NEG)
        keep = jnp.concatenate([keep, jnp.zeros((dh - n_sel, tq), F32)], axis=0).T.astype(BF16)
        q_sel = jnp.concatenate([q4, jnp.concatenate([keep] * hpg, axis=0)], axis=1)
        return o_cmp, q4, q_sel

    o_cmp, q4, q_sel = zip(*[select(g) for g in range(ng)])

    def sel_tile(j, carry, last):
        k0 = pl.multiple_of(j * tk, tk)
        onehot = et_ref[pl.ds(k0, tk), :]
        out = []
        for g in range(ng):
            m_i, acc = carry[g]
            k_aug = jnp.concatenate([ks_ref[pl.ds(k0, tk), gcols(g)], onehot], axis=1)
            s = _dot_nt(q_sel[g], k_aug)
            if last:
                causal = (k0 + _iota((tq, tk), 1)) <= tpos
                s = jnp.where(causal[None], s.reshape(hpg, tq, tk), NEG).reshape(rows, tk)
            m_new = jnp.maximum(m_i, jnp.max(s, axis=-1, keepdims=True))
            alpha = jnp.exp2((m_i - m_new) * c)
            e = jnp.exp2((s - m_new) * c).astype(BF16)
            v_aug = jnp.concatenate([vs_ref[pl.ds(k0, tk), gcols(g)], jnp.ones((tk, dh), BF16)], axis=1)
            out.append((m_new, alpha * acc + _dot(e, v_aug)))
        return tuple(out)

    j_last = q0 // tk
    init = tuple((jnp.full((rows, 1), NEG, F32), jnp.zeros((rows, 2 * dh), F32)) for _ in range(ng))
    carry = lax.fori_loop(0, j_last // 2,
                          lambda p, cr: sel_tile(2 * p + 1, sel_tile(2 * p, cr, False), False), init)
    carry = lax.fori_loop(j_last // 2 * 2, j_last, lambda j, cr: sel_tile(j, cr, False), carry)
    carry = sel_tile(j_last, carry, True)

    band = WINDOW + tq
    k0w = pl.multiple_of(jnp.maximum(q0 - WINDOW, 0), tq)
    eye = jnp.where(_iota((rows, dh), 1) == (_iota((rows, dh), 0) & (tq - 1)), 1.0, 0.0).astype(BF16)
    for g in range(ng):
        acc = carry[g][1]
        o_sel = acc[:, :dh] / jnp.maximum(acc[:, dh:], 1e-30)
        q_win = jnp.concatenate([q4[g], eye], axis=1)
        k_aug = jnp.concatenate([kw_ref[pl.ds(k0w, band), gcols(g)], wb_ref[...]], axis=1)
        sw = _dot_nt(q_win, k_aug)
        e = jnp.exp2((sw - jnp.max(sw, axis=-1, keepdims=True)) * c).astype(BF16)
        v_aug = jnp.concatenate([vw_ref[pl.ds(k0w, band), gcols(g)], jnp.ones((band, dh), BF16)], axis=1)
        pv = _dot(e, v_aug)
        o_win = pv[:, :dh] / jnp.maximum(pv[:, dh:], 1e-30)

        gt = gt_ref[g]
        for h in range(hpg):
            g_cmp, g_sel, g_win = (gt[:, 3 * h + r:3 * h + r + 1] for r in range(3))
            hs = slice(h * tq, (h + 1) * tq)
            o_ref[:, (g * hpg + h) * dh:(g * hpg + h + 1) * dh] = (
                g_cmp * o_cmp[g][hs] + g_sel * o_sel[hs] + g_win * o_win[hs]).astype(o_ref.dtype)


def _nsa_attention(proj, kv_cmp, gates, seq):
    b = proj.shape[0]
    g, hpg, dh = NSA_KV_GROUPS, NSA_HPG, NSA_HEAD_DIM
    tq, tk = NSA_Q_TILE, NSA_KEY_TILE
    n_cmp = kv_cmp.shape[3]
    n_sel = seq // SEL_BLOCK
    d_q = g * hpg * dh
    col_q, col_slc, col_win = 0, d_q // dh, (d_q + 2 * g * dh) // dh

    sub = jnp.arange(n_cmp)[None, :] // (SEL_BLOCK // CMP_STRIDE)
    nxt = (jnp.arange(n_cmp)[None, :] + 1) // (SEL_BLOCK // CMP_STRIDE)
    rows = jnp.arange(n_sel)[:, None]
    agg = ((sub == rows).astype(F32) + (nxt == rows).astype(F32)).astype(BF16)
    assert n_sel <= dh
    block_onehot = (jnp.arange(seq)[:, None] // SEL_BLOCK == jnp.arange(dh)[None, :]).astype(BF16)
    band = WINDOW + tq
    n_pat = WINDOW // tq + 1
    pat_q0 = jnp.arange(n_pat)[:, None, None] * tq
    key = jnp.maximum(pat_q0 - WINDOW, 0) + jnp.arange(band)[None, :, None]
    qry = pat_q0 + jnp.arange(tq)[None, None, :]
    win_bias = jnp.where((key <= qry) & (key > qry - WINDOW), 0.0, NEG).astype(BF16)

    ng = NSA_GROUPS_PER_STEP
    assert g % ng == 0 and col_slc % ng == 0 and col_win % ng == 0
    kv_spec = lambda col: pl.BlockSpec((None, seq, ng * dh), lambda bi, gi, i: (bi, 0, col // ng + gi))
    cmp_spec = lambda kv: pl.BlockSpec((None, None, ng, n_cmp, dh), lambda bi, gi, i: (kv, bi, gi, 0, 0))
    return pl.pallas_call(
        functools.partial(_nsa_kernel, tq=tq, tk=tk),
        out_shape=jax.ShapeDtypeStruct((b, seq, d_q), BF16),
        grid=(b, g // ng, seq // tq),
        in_specs=[pl.BlockSpec((None, tq, ng * hpg * dh), lambda bi, gi, i: (bi, i, col_q + gi)),
                  kv_spec(col_slc), kv_spec(col_slc + g), kv_spec(col_win), kv_spec(col_win + g),
                  cmp_spec(0), cmp_spec(1),
                  pl.BlockSpec((None, ng, tq, 3 * hpg), lambda bi, gi, i: (bi, gi, i, 0)),
                  pl.BlockSpec((n_sel, n_cmp), lambda bi, gi, i: (0, 0)),
                  pl.BlockSpec((seq, dh), lambda bi, gi, i: (0, 0)),
                  pl.BlockSpec((None, band, tq), lambda bi, gi, i: (jnp.minimum(i, n_pat - 1), 0, 0))],
        out_specs=pl.BlockSpec((None, tq, ng * hpg * dh), lambda bi, gi, i: (bi, i, gi)),
        compiler_params=_params("parallel", "parallel", "arbitrary"),
        name="nsa_attention",
    )(proj, proj, proj, proj, proj, kv_cmp, kv_cmp, gates, agg, block_onehot, win_bias)


def _nsa_layer(x2, mod, norm_g, p, batch, seq):
    m, d = x2.shape
    g, hpg, dh = NSA_KV_GROUPS, NSA_HPG, NSA_HEAD_DIM
    w_a = jnp.concatenate([p["w_q"], p["w_kv_slc"], p["w_kv_win"]], axis=1).astype(BF16)
    n_gate = p["w_gate"].shape[1]
    gate_pad = 128 - n_gate
    w_b = jnp.concatenate([p["w_kv_cmp"], jnp.pad(p["w_gate"], ((0, 0), (0, gate_pad)))], axis=1).astype(BF16)
    n_cmp_cols = p["w_kv_cmp"].shape[1]
    b_b = jnp.concatenate([jnp.zeros((n_cmp_cols,), F32), jnp.pad(p["b_gate"], (0, gate_pad))])
    proj = _nm_mm(x2, mod, norm_g, w_a, jnp.zeros((w_a.shape[1],), F32), seq, BF16,
                  tm=PROJ_TILE[0], tn=PROJ_TILE[1])
    proj_b = _nm_mm(x2, mod, norm_g, w_b, b_b, seq, F32, tm=ROW_TILE, tn=w_b.shape[1], sig_from=n_cmp_cols)
    proj_b3 = proj_b.reshape(batch, seq, proj_b.shape[1])
    kv_cmp = _nsa_compress(proj_b3, p["cmp_pe"], p["cmp_w1"], p["cmp_w2"])
    gates = proj_b3[:, :, n_cmp_cols:n_cmp_cols + n_gate].reshape(batch, seq, g, 3 * hpg)
    gates = jnp.transpose(gates, (0, 2, 1, 3))
    o = _nsa_attention(proj.reshape(batch, seq, proj.shape[1]), kv_cmp, gates, seq)
    return _mm_res(o.reshape(m, d), p["w_o"].astype(BF16), x2, mod, seq)


def _rope(x, cos2, sin2):
    half = x.shape[1] // 2
    swapped = jnp.concatenate([x[:, half:], x[:, :half]], axis=-1)
    return x * cos2 + swapped * sin2


def _mla_proj_kernel(x_ref, g_ref, sc_ref, sh_ref, w_ref, qg_ref, kvg_ref, cos_ref, sin_ref,
                     wuq_ref, wuk_ref, wuv_ref, q_ref, k_ref, v_ref):
    h = _norm_mod(x_ref[...], g_ref[...], sc_ref[...], sh_ref[...]).astype(BF16)
    y = _dot(h, w_ref[...])
    r0, r1, r2 = MLA_Q_RANK, MLA_Q_RANK + MLA_KV_RANK, MLA_Q_RANK + MLA_KV_RANK + MLA_ROPE_DIM
    cos2, sin2 = cos_ref[...], sin_ref[...]
    cq = _rms(y[:, :r0], qg_ref[...]).astype(BF16)
    ckv = _rms(y[:, r0:r1], kvg_ref[...]).astype(BF16)
    kr = _rope(y[:, r1:r2], cos2, sin2).astype(BF16)
    for hd in range(q_ref.shape[0]):
        yq = _dot(cq, wuq_ref[hd])
        qr = _rope(yq[:, MLA_NOPE_DIM:], cos2, sin2)
        q_ref[hd] = jnp.concatenate([yq[:, :MLA_NOPE_DIM], qr], axis=-1).astype(BF16)
        k_ref[hd] = jnp.concatenate([_dot(ckv, wuk_ref[hd]).astype(BF16), kr], axis=-1)
        v_ref[hd] = _dot(ckv, wuv_ref[hd]).astype(BF16)


def _mla_attn_kernel(q_ref, k_ref, v_ref, o_ref, *, tq, tk):
    dqk = q_ref.shape[2]
    nh, _, dv = v_ref.shape
    assert tq == tk
    c = dqk ** -0.5 * LOG2E
    ones = jnp.ones((tk, dv), BF16)

    def tile(j, carry, diagonal):
        k0 = pl.multiple_of(j * tk, tk)
        out = []
        for h in range(nh):
            m_i, acc = carry[h]
            s = _dot_nt(q_ref[h], k_ref[h, pl.ds(k0, tk), :])
            if diagonal:
                s = jnp.where(_iota((tq, tk), 1) <= _iota((tq, tk), 0), s, NEG)
            m_new = jnp.maximum(m_i, jnp.max(s, axis=-1, keepdims=True))
            alpha = jnp.exp2((m_i - m_new) * c)
            e = jnp.exp2((s - m_new) * c).astype(BF16)
            v_aug = jnp.concatenate([v_ref[h, pl.ds(k0, tk), :], ones], axis=1)
            out.append((m_new, alpha * acc + _dot(e, v_aug)))
        return tuple(out)

    i = pl.program_id(2)
    init = tuple((jnp.full((tq, 1), NEG, F32), jnp.zeros((tq, 2 * dv), F32)) for _ in range(nh))
    carry = lax.fori_loop(0, i // 2, lambda p, cr: tile(2 * p + 1, tile(2 * p, cr, False), False), init)
    carry = lax.fori_loop(i // 2 * 2, i, lambda j, cr: tile(j, cr, False), carry)
    carry = tile(i, carry, True)
    for h in range(nh):
        acc = carry[h][1]
        o_ref[:, h * dv:(h + 1) * dv] = (acc[:, :dv] / jnp.maximum(acc[:, dv:], 1e-30)).astype(o_ref.dtype)


def _mla_layer(x2, mod, norm_g, p, batch, seq):
    m, d = x2.shape
    hh, dn, dr, dv = MLA_HEADS, MLA_NOPE_DIM, MLA_ROPE_DIM, MLA_V_DIM
    rq, rkv = MLA_Q_RANK, MLA_KV_RANK
    tm = ROW_TILE
    tps = seq // tm

    pos = jnp.arange(seq, dtype=F32)
    inv_freq = ROPE_THETA ** (-jnp.arange(0, dr, 2, dtype=F32) / dr)
    ang = pos[:, None] * inv_freq[None, :]
    cos, sin = jnp.cos(ang), jnp.sin(ang)
    cos2 = jnp.concatenate([cos, cos], axis=-1)
    sin2 = jnp.concatenate([-sin, sin], axis=-1)
    rope_spec = pl.BlockSpec((tm, dr), lambda i, *_: (i % tps, 0))

    n_down = rq + rkv + dr
    n_pad = -n_down % 128
    w_down = jnp.pad(jnp.concatenate([p["w_dq"], p["w_dkv"]], axis=1), ((0, 0), (0, n_pad))).astype(BF16)
    dqk = dn + dr
    w_uq = jnp.transpose(p["w_uq"].reshape(rq, hh, dqk), (1, 0, 2)).astype(BF16)
    w_uk = jnp.transpose(p["w_uk"].reshape(rkv, hh, dn), (1, 0, 2)).astype(BF16)
    w_uv = jnp.transpose(p["w_uv"].reshape(rkv, hh, dv), (1, 0, 2)).astype(BF16)
    whole = lambda a: pl.BlockSpec(a.shape, lambda i: (0,) * a.ndim)
    head_out = lambda width: pl.BlockSpec((None, hh, tm, width), lambda i: (i // tps, 0, i % tps, 0))
    qf, kf, vf = pl.pallas_call(
        _mla_proj_kernel,
        out_shape=(jax.ShapeDtypeStruct((batch, hh, seq, dqk), BF16),
                   jax.ShapeDtypeStruct((batch, hh, seq, dqk), BF16),
                   jax.ShapeDtypeStruct((batch, hh, seq, dv), BF16)),
        grid=(m // tm,),
        in_specs=[pl.BlockSpec((tm, d), lambda i: (i, 0)),
                  _row_spec(d, False),
                  _mod_spec(1, tps, d, False), _mod_spec(0, tps, d, False),
                  whole(w_down), _row_spec(rq, False), _row_spec(rkv, False), rope_spec, rope_spec,
                  whole(w_uq), whole(w_uk), whole(w_uv)],
        out_specs=(head_out(dqk), head_out(dqk), head_out(dv)),
        compiler_params=_params("parallel"),
        name="mla_proj",
    )(x2, norm_g.reshape(1, d), mod, mod, w_down, p["q_norm_g"].reshape(1, rq),
      p["kv_norm_g"].reshape(1, rkv), cos2, sin2, w_uq, w_uk, w_uv)

    tq = tk = MLA_SEQ_TILE
    nh = MLA_HEADS_PER_STEP
    o = pl.pallas_call(
        functools.partial(_mla_attn_kernel, tq=tq, tk=tk),
        out_shape=jax.ShapeDtypeStruct((batch, seq, hh * dv), BF16),
        grid=(batch, hh // nh, seq // tq),
        in_specs=[pl.BlockSpec((None, nh, tq, dqk), lambda b, h, i: (b, h, i, 0)),
                  pl.BlockSpec((None, nh, seq, dqk), lambda b, h, i: (b, h, 0, 0)),
                  pl.BlockSpec((None, nh, seq, dv), lambda b, h, i: (b, h, 0, 0))],
        out_specs=pl.BlockSpec((None, tq, nh * dv), lambda b, h, i: (b, i, h)),
        compiler_params=_params("parallel", "parallel", "arbitrary"),
        name="mla_attention",
    )(qf, kf, vf)
    return _mm_res(o.reshape(m, hh * dv), p["w_o"].astype(BF16), x2, mod, seq)


def kernel(x, c, ada_w, ada_b, norm1_g, norm2_g, mlp_w1, mlp_w2, final_g, conv_w_in, conv_b_in, conv_w_dw, conv_b_dw, conv_ln_g, conv_ln_b, conv_w_out, conv_b_out, nsa_w_q, nsa_w_kv_cmp, nsa_w_kv_slc, nsa_w_kv_win, nsa_cmp_pe, nsa_cmp_w1, nsa_cmp_w2, nsa_w_gate, nsa_b_gate, nsa_w_o, pool_w, pool_b, pool_scale, mla_w_dq, mla_q_norm_g, mla_w_uq, mla_w_dkv, mla_kv_norm_g, mla_w_uk, mla_w_uv, mla_w_o):
    batch, seq, d = x.shape
    depth = ada_w.shape[0]
    n_mixers = 4
    mods = _ada_mod(c, ada_w, ada_b)
    x2 = x.reshape(batch * seq, d)
    w1, w2 = mlp_w1[0].astype(BF16), mlp_w2[0].astype(BF16)
    for i in range(depth):
        kind, u = i % n_mixers, i // n_mixers
        mod = mods[i].reshape(batch * 6, 1, d)
        if kind == 0:
            p = dict(w_in=conv_w_in[u], b_in=conv_b_in[u], w_dw=conv_w_dw[u], b_dw=conv_b_dw[u],
                     ln_g=conv_ln_g[u], ln_b=conv_ln_b[u], w_out=conv_w_out[u], b_out=conv_b_out[u])
            x2 = _conv_layer(x2, mod, norm1_g[i], p, seq)
        elif kind == 1:
            p = dict(w_q=nsa_w_q[u], w_kv_cmp=nsa_w_kv_cmp[u], w_kv_slc=nsa_w_kv_slc[u],
                     w_kv_win=nsa_w_kv_win[u], cmp_pe=nsa_cmp_pe[u], cmp_w1=nsa_cmp_w1[u],
                     cmp_w2=nsa_cmp_w2[u], w_gate=nsa_w_gate[u], b_gate=nsa_b_gate[u], w_o=nsa_w_o[u])
            x2 = _nsa_layer(x2, mod, norm1_g[i], p, batch, seq)
        elif kind == 2:
            p = dict(w=pool_w[u], b=pool_b[u], scale=pool_scale[u])
            x2 = _pool_layer(x2, mod, norm1_g[i], p, seq)
        else:
            p = dict(w_dq=mla_w_dq[u], q_norm_g=mla_q_norm_g[u], w_uq=mla_w_uq[u], w_dkv=mla_w_dkv[u],
                     kv_norm_g=mla_kv_norm_g[u], w_uk=mla_w_uk[u], w_uv=mla_w_uv[u], w_o=mla_w_o[u])
            x2 = _mla_layer(x2, mod, norm1_g[i], p, batch, seq)
        if i == depth - 1:
            x2 = _mlp(x2, mod, norm2_g[i], w1, w2, seq, final_g=final_g)
        else:
            x2, w1, w2 = _mlp(x2, mod, norm2_g[i], w1, w2, seq, next_f32=(mlp_w1, mlp_w2, i + 1))
    return x2.reshape(batch, seq, d)
```

```python
import functools

import jax
import jax.numpy as jnp
from jax import lax
from jax.experimental import pallas as pl
from jax.experimental.pallas import tpu as pltpu

F32 = jnp.float32
BF16 = jnp.bfloat16

EPS = 1e-6
NEG = -1e30
FORCE = 1e30
LOG2E = 1.4426950408889634

CONV_WIDTH = 31
CONV_HALO = 32
NSA_HEADS = 16
NSA_HEAD_DIM = 128
NSA_KV_GROUPS = 4
NSA_HPG = NSA_HEADS // NSA_KV_GROUPS
NSA_GROUPS_PER_STEP = 4
CMP_BLOCK = 32
CMP_STRIDE = 16
SEL_BLOCK = 64
SEL_TOPK = 16
WINDOW = 512
POOL_WINDOWS = (2, 4, 8, 16)
POOL_HALO = 16
MLA_HEADS = 16
MLA_NOPE_DIM = 128
MLA_ROPE_DIM = 64
MLA_V_DIM = 128
MLA_Q_RANK = 512
MLA_KV_RANK = 256
ROPE_THETA = 10000.0

VMEM_LIMIT_BYTES = 56 * 1024 * 1024
LANES = 128
SUBLANES = 8

ROW_TILE = 512
MLP_FF_TILE = 1024
ADA_COL_TILE = 1024
PROJ_TILE = (1024, 1024)
GLU_TILE = (1024, 512)
CONV_TILE = (512, 1024)
POOL_ROW_TILE = 256
NSA_Q_TILE, NSA_KEY_TILE = 128, 512
MLA_SEQ_TILE = 512
MLA_HEADS_PER_STEP = 4


def _params(*semantics):
    return pltpu.CompilerParams(dimension_semantics=semantics, vmem_limit_bytes=VMEM_LIMIT_BYTES)


def _dot(a, b):
    return jnp.dot(a, b, preferred_element_type=F32)


def _dot_nt(a, b):
    return lax.dot_general(a, b, (((1,), (1,)), ((), ())), preferred_element_type=F32)


def _rms(x, g):
    return x * lax.rsqrt(jnp.mean(x * x, axis=-1, keepdims=True) + EPS) * g


def _norm_mod(x, g, sc, sh):
    return _rms(x, g) * (1.0 + sc) + sh


def _sigmoid(x):
    return 1.0 / (1.0 + jnp.exp(-x))


def _iota(shape, dim):
    return lax.broadcasted_iota(jnp.int32, shape, dim)


def _mod_spec(k, tiles_per_seq, width, tiled):
    if tiled:
        return pl.BlockSpec((None, 1, width), lambda i, j: ((i // tiles_per_seq) * 6 + k, 0, j))
    return pl.BlockSpec((None, 1, width), lambda i, *_: ((i // tiles_per_seq) * 6 + k, 0, 0))


def _row_spec(width, tiled):
    if tiled:
        return pl.BlockSpec((1, width), lambda i, j: (0, j))
    return pl.BlockSpec((1, width), lambda i, *_: (0, 0))


def _ada_kernel(c_ref, w_ref, b_ref, o_ref):
    c = c_ref[...]
    cs = c * _sigmoid(c)
    o_ref[...] = _dot(cs.astype(BF16), w_ref[...].astype(BF16)) + b_ref[...]


def _ada_mod(c, ada_w, ada_b):
    depth, d, n = ada_w.shape
    b = c.shape[0]
    bp = SUBLANES
    tn = ADA_COL_TILE
    cp = jnp.pad(c, ((0, bp - b), (0, 0)))
    out = pl.pallas_call(
        _ada_kernel,
        out_shape=jax.ShapeDtypeStruct((depth, bp, n), F32),
        grid=(depth, n // tn),
        in_specs=[pl.BlockSpec((bp, d), lambda l, j: (0, 0)),
                  pl.BlockSpec((None, d, tn), lambda l, j: (l, 0, j)),
                  pl.BlockSpec((None, 1, tn), lambda l, j: (l, 0, j))],
        out_specs=pl.BlockSpec((None, bp, tn), lambda l, j: (l, 0, j)),
        compiler_params=_params("parallel", "parallel"),
        name="ada_mod",
    )(cp, ada_w, ada_b.reshape(depth, 1, n))
    return out[:, :b]


def _with_next_tile_norm(x0_ref, xn_ref, g_ref, sc_ref, sh_ref, scn_ref, shn_ref, h_even, h_odd, body):
    i, j = pl.program_id(0), pl.program_id(1)
    rows = xn_ref.shape[0]

    @pl.when((i == 0) & (j == 0))
    def _():
        h_even[...] = _norm_mod(x0_ref[...], g_ref[...], sc_ref[...], sh_ref[...]).astype(BF16)

    def step(h_cur, h_next):
        body(h_cur)
        r0 = pl.multiple_of(j * rows, rows)
        h_next[pl.ds(r0, rows), :] = _norm_mod(
            xn_ref[...], g_ref[...], scn_ref[...], shn_ref[...]).astype(BF16)

    @pl.when(i % 2 == 0)
    def _():
        step(h_even, h_odd)

    @pl.when(i % 2 == 1)
    def _():
        step(h_odd, h_even)


def _next_tile_specs(m, tm, steps, d, tiles_per_seq, k_scale, k_shift):
    n_tiles = m // tm
    rows = tm // steps
    assert rows * steps == tm and rows % 16 == 0
    nxt = lambda i: jnp.minimum(i + 1, n_tiles - 1)
    mod_next = lambda k: pl.BlockSpec((None, 1, d), lambda i, j: ((nxt(i) // tiles_per_seq) * 6 + k, 0, 0))
    return (pl.BlockSpec((rows, d), lambda i, j: (nxt(i) * steps + j, 0)), mod_next(k_scale), mod_next(k_shift))


def _mlp_kernel(x_ref, xn_ref, g_ref, sc_ref, sh_ref, scn_ref, shn_ref, gate_ref, w1_ref, w2_ref, *rest, final):
    if final:
        fg_ref, o_ref, h_even, h_odd, acc_ref = rest
    else:
        w1f_ref, w2f_ref, o_ref, w1n_ref, w2n_ref, h_even, h_odd, acc_ref = rest
    f = pl.program_id(1)

    @pl.when(f == 0)
    def _():
        acc_ref[...] = jnp.zeros_like(acc_ref)

    def body(h_cur):
        a = _dot(h_cur[...], w1_ref[...])
        a = jnp.square(jnp.maximum(a, 0.0)).astype(BF16)
        acc_ref[...] += _dot(a, w2_ref[...])
        if not final:
            w1n_ref[...] = w1f_ref[...].astype(BF16)
            w2n_ref[...] = w2f_ref[...].astype(BF16)

    _with_next_tile_norm(x_ref, xn_ref, g_ref, sc_ref, sh_ref, scn_ref, shn_ref, h_even, h_odd, body)

    @pl.when(f == pl.num_programs(1) - 1)
    def _():
        out = x_ref[...] + gate_ref[...] * acc_ref[...]
        if final:
            out = _rms(out, fg_ref[...])
        o_ref[...] = out


def _mlp(x2, mod, norm_g, w1, w2, seq, next_f32=None, final_g=None):
    m, d = x2.shape
    dff = w1.shape[1]
    tm, tf = ROW_TILE, MLP_FF_TILE
    tps = seq // tm
    final = final_g is not None
    nt, nf = m // tm, dff // tf
    xn_spec, scn_spec, shn_spec = _next_tile_specs(m, tm, nf, d, tps, 4, 3)
    in_specs = [pl.BlockSpec((tm, d), lambda i, f: (i, 0)),
                xn_spec,
                _row_spec(d, False),
                _mod_spec(4, tps, d, False), _mod_spec(3, tps, d, False), scn_spec, shn_spec,
                _mod_spec(5, tps, d, False),
                pl.BlockSpec((d, tf), lambda i, f: (0, f)),
                pl.BlockSpec((tf, d), lambda i, f: (f, 0))]
    args = [x2, x2, norm_g.reshape(1, d), mod, mod, mod, mod, mod, w1, w2]
    out_shape = [jax.ShapeDtypeStruct((m, d), F32)]
    out_specs = [pl.BlockSpec((tm, d), lambda i, f: (i, 0))]
    if final:
        in_specs.append(_row_spec(d, False))
        args.append(final_g.reshape(1, d))
    else:
        w1_all, w2_all, nxt = next_f32
        steps = nt * nf
        r1, c1 = 2 * d // steps, dff // 2
        r2 = dff // steps
        assert r1 * (steps // 2) == d and r2 * steps == dff and r1 % 16 == 0 and r2 % 16 == 0
        step = lambda i, f: i * nf + f
        in_specs += [pl.BlockSpec((None, r1, c1), lambda i, f: (nxt, step(i, f) // 2, step(i, f) % 2)),
                     pl.BlockSpec((None, r2, d), lambda i, f: (nxt, step(i, f), 0))]
        args += [w1_all, w2_all]
        out_shape += [jax.ShapeDtypeStruct((d, dff), BF16), jax.ShapeDtypeStruct((dff, d), BF16)]
        out_specs += [pl.BlockSpec((r1, c1), lambda i, f: (step(i, f) // 2, step(i, f) % 2)),
                      pl.BlockSpec((r2, d), lambda i, f: (step(i, f), 0))]
    out = pl.pallas_call(
        functools.partial(_mlp_kernel, final=final),
        out_shape=tuple(out_shape),
        grid=(nt, nf),
        in_specs=in_specs,
        out_specs=tuple(out_specs),
        scratch_shapes=[pltpu.VMEM((tm, d), BF16), pltpu.VMEM((tm, d), BF16), pltpu.VMEM((tm, d), F32)],
        compiler_params=_params("arbitrary", "arbitrary"),
        name="mlp",
    )(*args)
    return out[0] if final else out


def _nm_mm_kernel(x0_ref, xn_ref, g_ref, sc_ref, sh_ref, scn_ref, shn_ref, w_ref, b_ref, o_ref, h_even, h_odd,
                  *, sig_from, tn):
    def body(h_cur):
        y = _dot(h_cur[...], w_ref[...]) + b_ref[...]
        if sig_from is not None:
            col = pl.program_id(1) * tn + _iota(y.shape, 1)
            y = jnp.where(col >= sig_from, _sigmoid(y), y)
        o_ref[...] = y.astype(o_ref.dtype)

    _with_next_tile_norm(x0_ref, xn_ref, g_ref, sc_ref, sh_ref, scn_ref, shn_ref, h_even, h_odd, body)


def _first_tile_specs(tm, d):
    first_mod = lambda k: pl.BlockSpec((None, 1, d), lambda i, j: (k, 0, 0))
    return pl.BlockSpec((tm, d), lambda i, j: (0, 0)), first_mod(1), first_mod(0)


def _nm_mm(x2, mod, norm_g, w, bias, seq, out_dtype, tm, tn, sig_from=None):
    m, d = x2.shape
    n = w.shape[1]
    tps = seq // tm
    x0_spec, sc0_spec, sh0_spec = _first_tile_specs(tm, d)
    xn_spec, scn_spec, shn_spec = _next_tile_specs(m, tm, n // tn, d, tps, 1, 0)
    return pl.pallas_call(
        functools.partial(_nm_mm_kernel, sig_from=sig_from, tn=tn),
        out_shape=jax.ShapeDtypeStruct((m, n), out_dtype),
        grid=(m // tm, n // tn),
        in_specs=[x0_spec, xn_spec, _row_spec(d, False), sc0_spec, sh0_spec, scn_spec, shn_spec,
                  pl.BlockSpec((d, tn), lambda i, j: (0, j)),
                  _row_spec(tn, True)],
        out_specs=pl.BlockSpec((tm, tn), lambda i, j: (i, j)),
        scratch_shapes=[pltpu.VMEM((tm, d), BF16), pltpu.VMEM((tm, d), BF16)],
        compiler_params=_params("arbitrary", "arbitrary"),
        name="norm_mod_matmul",
    )(x2, x2, norm_g.reshape(1, d), mod, mod, mod, mod, w, bias.reshape(1, n))


def _mm_res_kernel(a_ref, w_ref, x_ref, gate_ref, o_ref):
    o_ref[...] = x_ref[...] + gate_ref[...] * _dot(a_ref[...], w_ref[...])


def _mm_res(a, w, x2, mod, seq):
    m, k = a.shape
    n = w.shape[1]
    tm = ROW_TILE
    tps = seq // tm
    return pl.pallas_call(
        _mm_res_kernel,
        out_shape=jax.ShapeDtypeStruct((m, n), F32),
        grid=(m // tm,),
        in_specs=[pl.BlockSpec((tm, k), lambda i: (i, 0)),
                  pl.BlockSpec((k, n), lambda i: (0, 0)),
                  pl.BlockSpec((tm, n), lambda i: (i, 0)),
                  _mod_spec(2, tps, n, False)],
        out_specs=pl.BlockSpec((tm, n), lambda i: (i, 0)),
        compiler_params=_params("parallel"),
        name="matmul_residual",
    )(a, w, x2, mod)


def _glu_kernel(x0_ref, xn_ref, g_ref, sc_ref, sh_ref, scn_ref, shn_ref, wa_ref, wg_ref, ba_ref, bg_ref,
                o_ref, h_even, h_odd):
    def body(h_cur):
        h = h_cur[...]
        a = _dot(h, wa_ref[...]) + ba_ref[...]
        gt = _dot(h, wg_ref[...]) + bg_ref[...]
        o_ref[...] = a * _sigmoid(gt)

    _with_next_tile_norm(x0_ref, xn_ref, g_ref, sc_ref, sh_ref, scn_ref, shn_ref, h_even, h_odd, body)


def _conv_kernel(u_ref, uh_ref, wdw_ref, bdw_ref, lng_ref, lnb_ref, wo_ref, bo_ref, x_ref, gate_ref,
                 w1f_ref, w2f_ref, o_ref, w1n_ref, w2n_ref, ubuf, shifted, conv_ref, v_ref,
                 *, tm, tiles_per_seq):
    w1n_ref[...] = w1f_ref[...].astype(BF16)
    w2n_ref[...] = w2f_ref[...].astype(BF16)
    i = pl.program_id(0)
    d = u_ref.shape[1]
    rc = 64
    rn = 16
    span = tm + CONV_HALO - 8

    @pl.when(pl.program_id(1) == 0)
    def _():
        first = (i % tiles_per_seq) == 0
        ubuf[0:CONV_HALO, :] = jnp.where(first, 0.0, uh_ref[...])
        ubuf[CONV_HALO:, :] = u_ref[...]

        def column(ct, carry):
            cs = pl.ds(pl.multiple_of(ct * LANES, LANES), LANES)
            for r in range(1, 8):
                shifted[r - 1] = ubuf[r:r + span, cs]
            taps = [jnp.broadcast_to(wdw_ref[k:k + 1, cs], (8, LANES)) for k in range(CONV_WIDTH)]
            bias = jnp.broadcast_to(bdw_ref[:, cs], (8, LANES))

            def chunk(c, inner):
                r0 = pl.multiple_of(c * rc, rc)
                for v in range(rc // 8):
                    acc = bias
                    for k in range(CONV_WIDTH):
                        off = CONV_HALO - (CONV_WIDTH - 1) + k
                        q, r = divmod(off, 8)
                        start = pl.multiple_of(r0 + 8 * (q + v), 8)
                        rows = ubuf[pl.ds(start, 8), cs] if r == 0 else shifted[r - 1, pl.ds(start, 8), :]
                        acc = acc + taps[k] * rows
                    conv_ref[pl.ds(pl.multiple_of(r0 + 8 * v, 8), 8), cs] = acc
                return inner

            lax.fori_loop(0, tm // rc, chunk, 0)
            return carry

        lax.fori_loop(0, d // LANES, column, 0)

        def norm_chunk(c, carry):
            r0 = pl.multiple_of(c * rn, rn)
            acc = conv_ref[pl.ds(r0, rn), :]
            mu = jnp.mean(acc, axis=-1, keepdims=True)
            cen = acc - mu
            var = jnp.mean(cen * cen, axis=-1, keepdims=True)
            y = cen * lax.rsqrt(var + EPS) * lng_ref[...] + lnb_ref[...]
            v_ref[pl.ds(r0, rn), :] = (y * _sigmoid(y)).astype(BF16)
            return carry

        lax.fori_loop(0, tm // rn, norm_chunk, 0, unroll=4)

    y = _dot(v_ref[...], wo_ref[...]) + bo_ref[...]
    o_ref[...] = x_ref[...] + gate_ref[...] * y


def _conv_layer(x2, mod, norm_g, p, seq, mlp_f32):
    m, d = x2.shape
    tm, tn = GLU_TILE
    tps = seq // tm
    w_in = p["w_in"].astype(BF16)
    x0_spec, sc0_spec, sh0_spec = _first_tile_specs(tm, d)
    xn_spec, scn_spec, shn_spec = _next_tile_specs(m, tm, d // tn, d, tps, 1, 0)
    u = pl.pallas_call(
        _glu_kernel,
        out_shape=jax.ShapeDtypeStruct((m, d), F32),
        grid=(m // tm, d // tn),
        in_specs=[x0_spec, xn_spec, _row_spec(d, False), sc0_spec, sh0_spec, scn_spec, shn_spec,
                  pl.BlockSpec((d, tn), lambda i, j: (0, j)),
                  pl.BlockSpec((d, tn), lambda i, j: (0, j + d // tn)),
                  pl.BlockSpec((1, tn), lambda i, j: (0, j)),
                  pl.BlockSpec((1, tn), lambda i, j: (0, j + d // tn))],
        out_specs=pl.BlockSpec((tm, tn), lambda i, j: (i, j)),
        scratch_shapes=[pltpu.VMEM((tm, d), BF16), pltpu.VMEM((tm, d), BF16)],
        compiler_params=_params("arbitrary", "arbitrary"),
        name="conv_glu",
    )(x2, x2, norm_g.reshape(1, d), mod, mod, mod, mod, w_in, w_in,
      p["b_in"].reshape(1, 2 * d), p["b_in"].reshape(1, 2 * d))

    tm, tn = CONV_TILE
    tps = seq // tm
    hb = tm // CONV_HALO
    nt, nj = m // tm, d // tn
    w1_all, w2_all, layer = mlp_f32
    dff = w1_all.shape[2]
    steps = nt * nj
    r1, r2 = d // steps, dff // steps
    assert r1 * steps == d and r2 * steps == dff and r1 % 16 == 0 and r2 % 16 == 0
    step = lambda i, j: i * nj + j
    return pl.pallas_call(
        functools.partial(_conv_kernel, tm=tm, tiles_per_seq=tps),
        out_shape=(jax.ShapeDtypeStruct((m, d), F32), jax.ShapeDtypeStruct((d, dff), BF16),
                   jax.ShapeDtypeStruct((dff, d), BF16)),
        grid=(nt, nj),
        in_specs=[pl.BlockSpec((tm, d), lambda i, j: (i, 0)),
                  pl.BlockSpec((CONV_HALO, d), lambda i, j: (jnp.maximum(i * hb - 1, 0), 0)),
                  pl.BlockSpec((CONV_WIDTH, d), lambda i, j: (0, 0)),
                  _row_spec(d, False), _row_spec(d, False), _row_spec(d, False),
                  pl.BlockSpec((d, tn), lambda i, j: (0, j)),
                  _row_spec(tn, True),
                  pl.BlockSpec((tm, tn), lambda i, j: (i, j)),
                  _mod_spec(2, tps, tn, True),
                  pl.BlockSpec((None, r1, dff), lambda i, j: (layer, step(i, j), 0)),
                  pl.BlockSpec((None, r2, d), lambda i, j: (layer, step(i, j), 0))],
        out_specs=(pl.BlockSpec((tm, tn), lambda i, j: (i, j)),
                   pl.BlockSpec((r1, dff), lambda i, j: (step(i, j), 0)),
                   pl.BlockSpec((r2, d), lambda i, j: (step(i, j), 0))),
        scratch_shapes=[pltpu.VMEM((tm + CONV_HALO, d), F32),
                        pltpu.VMEM((7, tm + CONV_HALO - 8, LANES), F32),
                        pltpu.VMEM((tm, d), F32),
                        pltpu.VMEM((tm, d), BF16)],
        compiler_params=_params("parallel", "arbitrary"),
        name="conv_ln_out",
    )(u, u, p["w_dw"], p["b_dw"].reshape(1, d), p["ln_g"].reshape(1, d), p["ln_b"].reshape(1, d),
      p["w_out"].astype(BF16), p["b_out"].reshape(1, d), x2, mod, w1_all, w2_all)


def _pool_kernel(x_ref, xh_ref, g_ref, sc_ref, sh_ref, gate_ref, w_ref, b_ref, scale_ref, o_ref, hbuf,
                 *, tm, tiles_per_seq):
    i = pl.program_id(0)
    d = x_ref.shape[1]
    gd = d // len(POOL_WINDOWS)
    first = (i % tiles_per_seq) == 0
    g, sc, sh = g_ref[...], sc_ref[...], sh_ref[...]
    hbuf[0:POOL_HALO, :] = jnp.where(first, 0.0, _norm_mod(xh_ref[...], g, sc, sh))
    hbuf[POOL_HALO:, :] = _norm_mod(x_ref[...], g, sc, sh)
    tpos = (i % tiles_per_seq) * tm + _iota((tm, 1), 0)
    for gi, win in enumerate(POOL_WINDOWS):
        lo, hi = gi * gd, (gi + 1) * gd
        h = hbuf[POOL_HALO:, lo:hi]
        wsum = h
        for k in range(1, win):
            wsum = wsum + hbuf[POOL_HALO - k:POOL_HALO - k + tm, lo:hi]
        cnt = jnp.minimum(tpos + 1, win).astype(F32)
        dlt = wsum / cnt - h
        y = (_dot(dlt.astype(BF16), w_ref[gi]) + b_ref[:, lo:hi]) * scale_ref[:, lo:hi]
        o_ref[:, lo:hi] = x_ref[:, lo:hi] + gate_ref[:, lo:hi] * y


def _pool_layer(x2, mod, norm_g, p, seq):
    m, d = x2.shape
    tm = POOL_ROW_TILE
    tps = seq // tm
    hb = tm // POOL_HALO
    ng, gd = p["w"].shape[0], p["w"].shape[1]
    return pl.pallas_call(
        functools.partial(_pool_kernel, tm=tm, tiles_per_seq=tps),
        out_shape=jax.ShapeDtypeStruct((m, d), F32),
        grid=(m // tm,),
        in_specs=[pl.BlockSpec((tm, d), lambda i: (i, 0)),
                  pl.BlockSpec((POOL_HALO, d), lambda i: (jnp.maximum(i * hb - 1, 0), 0)),
                  _row_spec(d, False),
                  _mod_spec(1, tps, d, False), _mod_spec(0, tps, d, False), _mod_spec(2, tps, d, False),
                  pl.BlockSpec((ng, gd, gd), lambda i: (0, 0, 0)),
                  _row_spec(d, False), _row_spec(d, False)],
        out_specs=pl.BlockSpec((tm, d), lambda i: (i, 0)),
        scratch_shapes=[pltpu.VMEM((tm + POOL_HALO, d), F32)],
        compiler_params=_params("parallel"),
        name="pool_mixer",
    )(x2, x2, norm_g.reshape(1, d), mod, mod, mod, p["w"].astype(BF16),
      p["b"].reshape(1, d), p["scale"].reshape(1, d))


def _cmp_kernel(x_ref, pe_ref, w1_ref, w2_ref, o_ref, *, n_sub):
    dh = x_ref.shape[1]
    acc_a = jnp.zeros((n_sub, dh), F32)
    acc_b = jnp.zeros((n_sub, dh), F32)
    for p in range(CMP_STRIDE):
        xp = x_ref[pl.ds(p, n_sub, stride=CMP_STRIDE), :]
        acc_a = acc_a + _dot((xp + pe_ref[p:p + 1, :]).astype(BF16), w1_ref[p])
        q = CMP_STRIDE + p
        acc_b = acc_b + _dot((xp + pe_ref[q:q + 1, :]).astype(BF16), w1_ref[q])
    hid = acc_a + pltpu.roll(acc_b, n_sub - 1, 0)
    hid = hid * _sigmoid(hid)
    o_ref[...] = _dot(hid.astype(BF16), w2_ref[...])


def _nsa_compress(kvc3, pe, w1, w2):
    b, s, _ = kvc3.shape
    g, dh = NSA_KV_GROUPS, NSA_HEAD_DIM
    n_sub = s // CMP_STRIDE
    return pl.pallas_call(
        functools.partial(_cmp_kernel, n_sub=n_sub),
        out_shape=jax.ShapeDtypeStruct((2, b, g, n_sub, dh), F32),
        grid=(2, b, g),
        in_specs=[pl.BlockSpec((None, s, dh), lambda kv, bi, gi: (bi, 0, kv * g + gi)),
                  pl.BlockSpec((None, CMP_BLOCK, dh), lambda kv, bi, gi: (kv, 0, 0)),
                  pl.BlockSpec((None, CMP_BLOCK, dh, dh), lambda kv, bi, gi: (kv, 0, 0, 0)),
                  pl.BlockSpec((None, dh, dh), lambda kv, bi, gi: (kv, 0, 0))],
        out_specs=pl.BlockSpec((None, None, None, n_sub, dh), lambda kv, bi, gi: (kv, bi, gi, 0, 0)),
        compiler_params=_params("parallel", "parallel", "parallel"),
        name="nsa_compress",
    )(kvc3, pe, w1.astype(BF16), w2.astype(BF16))


def _sel_block_of(pos):
    assert SEL_BLOCK & (SEL_BLOCK - 1) == 0
    return jnp.right_shift(pos, SEL_BLOCK.bit_length() - 1)


def _split3(x):
    hi = x.astype(BF16)
    r = x - hi.astype(F32)
    mid = r.astype(BF16)
    lo = (r - mid.astype(F32)).astype(BF16)
    return hi, mid, lo


def _nsa_kernel(q_ref, ks_ref, vs_ref, kw_ref, vw_ref, kc_ref, vc_ref, gt_ref, at_ref, et_ref, wb_ref, o_ref,
                *, tq, tk):
    hpg, dh = NSA_HPG, NSA_HEAD_DIM
    assert tq == dh
    ng, n_cmp, _ = kc_ref.shape
    n_sel = at_ref.shape[0]
    rows = hpg * tq
    q0 = pl.program_id(2) * tq
    c = dh ** -0.5 * LOG2E
    tpos = q0 + _iota((tq, 1), 0)
    gcols = lambda g: slice(g * dh, (g + 1) * dh)

    def select(g):
        q = q_ref[:, g * hpg * dh:(g + 1) * hpg * dh]
        q4 = jnp.concatenate([q[:, h * dh:(h + 1) * dh] for h in range(hpg)], axis=0)
        sc = _dot_nt(q4, kc_ref[g].astype(BF16)).reshape(hpg, tq, n_cmp)
        cvalid = (_iota((tq, n_cmp), 1) * CMP_STRIDE + (CMP_BLOCK - 1)) <= tpos
        sc = sc + jnp.where(cvalid, 0.0, NEG)[None]
        e = jnp.exp2((sc - jnp.max(sc, axis=-1, keepdims=True)) * c)
        l = jnp.sum(e, axis=-1, keepdims=True)
        row_ok = jnp.where(tpos >= CMP_BLOCK - 1, 1.0, 0.0)
        p_cmp = e * (row_ok / jnp.maximum(l, 1e-30))
        o_cmp = _dot(p_cmp.reshape(rows, n_cmp).astype(BF16), vc_ref[g].astype(BF16))
        pg = p_cmp[0]
        for h in range(1, hpg):
            pg = pg + p_cmp[h]

        at = at_ref[...]
        imp_t = sum(_dot_nt(at, piece) for piece in _split3(pg))
        blk = _iota((n_sel, tq), 0)
        cur = _sel_block_of(q0 + _iota((n_sel, tq), 1))
        forced = (blk == 0) | (blk == cur) | (blk == cur - 1)
        val = jnp.where(forced, FORCE, jnp.where(blk > cur, NEG, imp_t))
        ranks = []
        for r0 in range(0, n_sel, 8):
            mine = val[r0:r0 + 8, :]
            rank = jnp.zeros((8, tq), F32)
            for mm in range(n_sel):
                other = val[mm:mm + 1, :]
                if mm < r0:
                    rank = rank + jnp.where(other >= mine, 1.0, 0.0)
                elif mm >= r0 + 8:
                    rank = rank + jnp.where(other > mine, 1.0, 0.0)
                else:
                    wins_tie = jnp.where(_iota((8, tq), 0) + r0 > mm, 1.0, 0.0)
                    rank = rank + jnp.where(other > mine, 1.0, jnp.where(other == mine, wins_tie, 0.0))
            ranks.append(rank)
        rank = jnp.concatenate(ranks, axis=0)
        keep = jnp.where(rank < SEL_TOPK, jnp.where(blk <= cur, 0.0, NEG), NEG)
        keep = jnp.concatenate([keep, jnp.zeros((dh - n_sel, tq), F32)], axis=0).T.astype(BF16)
        q_sel = jnp.concatenate([q4, jnp.concatenate([keep] * hpg, axis=0)], axis=1)
        return o_cmp, q4, q_sel

    o_cmp, q4, q_sel = zip(*[select(g) for g in range(ng)])

    def sel_tile(j, carry, last):
        k0 = pl.multiple_of(j * tk, tk)
        onehot = et_ref[pl.ds(k0, tk), :]
        out = []
        for g in range(ng):
            m_i, acc = carry[g]
            k_aug = jnp.concatenate([ks_ref[pl.ds(k0, tk), gcols(g)], onehot], axis=1)
            s = _dot_nt(q_sel[g], k_aug)
            if last:
                causal = (k0 + _iota((tq, tk), 1)) <= tpos
                s = jnp.where(causal[None], s.reshape(hpg, tq, tk), NEG).reshape(rows, tk)
            m_new = jnp.maximum(m_i, jnp.max(s, axis=-1, keepdims=True))
            alpha = jnp.exp2((m_i - m_new) * c)
            e = jnp.exp2((s - m_new) * c).astype(BF16)
            v_aug = jnp.concatenate([vs_ref[pl.ds(k0, tk), gcols(g)], jnp.ones((tk, dh), BF16)], axis=1)
            out.append((m_new, alpha * acc + _dot(e, v_aug)))
        return tuple(out)

    j_last = q0 // tk
    init = tuple((jnp.full((rows, 1), NEG, F32), jnp.zeros((rows, 2 * dh), F32)) for _ in range(ng))
    carry = lax.fori_loop(0, j_last // 2,
                          lambda p, cr: sel_tile(2 * p + 1, sel_tile(2 * p, cr, False), False), init)
    carry = lax.fori_loop(j_last // 2 * 2, j_last, lambda j, cr: sel_tile(j, cr, False), carry)
    carry = sel_tile(j_last, carry, True)

    band = WINDOW + tq
    k0w = pl.multiple_of(jnp.maximum(q0 - WINDOW, 0), tq)
    eye = jnp.where(_iota((rows, dh), 1) == (_iota((rows, dh), 0) & (tq - 1)), 1.0, 0.0).astype(BF16)
    for g in range(ng):
        acc = carry[g][1]
        o_sel = acc[:, :dh] / jnp.maximum(acc[:, dh:], 1e-30)
        q_win = jnp.concatenate([q4[g], eye], axis=1)
        k_aug = jnp.concatenate([kw_ref[pl.ds(k0w, band), gcols(g)], wb_ref[...]], axis=1)
        sw = _dot_nt(q_win, k_aug)
        e = jnp.exp2((sw - jnp.max(sw, axis=-1, keepdims=True)) * c).astype(BF16)
        v_aug = jnp.concatenate([vw_ref[pl.ds(k0w, band), gcols(g)], jnp.ones((band, dh), BF16)], axis=1)
        pv = _dot(e, v_aug)
        o_win = pv[:, :dh] / jnp.maximum(pv[:, dh:], 1e-30)

        gt = gt_ref[g]
        for h in range(hpg):
            g_cmp, g_sel, g_win = (gt[:, 3 * h + r:3 * h + r + 1] for r in range(3))
            hs = slice(h * tq, (h + 1) * tq)
            o_ref[:, (g * hpg + h) * dh:(g * hpg + h + 1) * dh] = (
                g_cmp * o_cmp[g][hs] + g_sel * o_sel[hs] + g_win * o_win[hs]).astype(o_ref.dtype)


def _nsa_attention(proj, kv_cmp, gates, seq):
    b = proj.shape[0]
    g, hpg, dh = NSA_KV_GROUPS, NSA_HPG, NSA_HEAD_DIM
    tq, tk = NSA_Q_TILE, NSA_KEY_TILE
    n_cmp = kv_cmp.shape[3]
    n_sel = seq // SEL_BLOCK
    d_q = g * hpg * dh
    col_q, col_slc, col_win = 0, d_q // dh, (d_q + 2 * g * dh) // dh

    sub = jnp.arange(n_cmp)[None, :] // (SEL_BLOCK // CMP_STRIDE)
    nxt = (jnp.arange(n_cmp)[None, :] + 1) // (SEL_BLOCK // CMP_STRIDE)
    rows = jnp.arange(n_sel)[:, None]
    agg = ((sub == rows).astype(F32) + (nxt == rows).astype(F32)).astype(BF16)
    assert n_sel <= dh
    block_onehot = (jnp.arange(seq)[:, None] // SEL_BLOCK == jnp.arange(dh)[None, :]).astype(BF16)
    band = WINDOW + tq
    n_pat = WINDOW // tq + 1
    pat_q0 = jnp.arange(n_pat)[:, None, None] * tq
    key = jnp.maximum(pat_q0 - WINDOW, 0) + jnp.arange(band)[None, :, None]
    qry = pat_q0 + jnp.arange(tq)[None, None, :]
    win_bias = jnp.where((key <= qry) & (key > qry - WINDOW), 0.0, NEG).astype(BF16)

    ng = NSA_GROUPS_PER_STEP
    assert g % ng == 0 and col_slc % ng == 0 and col_win % ng == 0
    kv_spec = lambda col: pl.BlockSpec((None, seq, ng * dh), lambda bi, gi, i: (bi, 0, col // ng + gi))
    cmp_spec = lambda kv: pl.BlockSpec((None, None, ng, n_cmp, dh), lambda bi, gi, i: (kv, bi, gi, 0, 0))
    return pl.pallas_call(
        functools.partial(_nsa_kernel, tq=tq, tk=tk),
        out_shape=jax.ShapeDtypeStruct((b, seq, d_q), BF16),
        grid=(b, g // ng, seq // tq),
        in_specs=[pl.BlockSpec((None, tq, ng * hpg * dh), lambda bi, gi, i: (bi, i, col_q + gi)),
                  kv_spec(col_slc), kv_spec(col_slc + g), kv_spec(col_win), kv_spec(col_win + g),
                  cmp_spec(0), cmp_spec(1),
                  pl.BlockSpec((None, ng, tq, 3 * hpg), lambda bi, gi, i: (bi, gi, i, 0)),
                  pl.BlockSpec((n_sel, n_cmp), lambda bi, gi, i: (0, 0)),
                  pl.BlockSpec((seq, dh), lambda bi, gi, i: (0, 0)),
                  pl.BlockSpec((None, band, tq), lambda bi, gi, i: (jnp.minimum(i, n_pat - 1), 0, 0))],
        out_specs=pl.BlockSpec((None, tq, ng * hpg * dh), lambda bi, gi, i: (bi, i, gi)),
        compiler_params=_params("parallel", "parallel", "arbitrary"),
        name="nsa_attention",
    )(proj, proj, proj, proj, proj, kv_cmp, kv_cmp, gates, agg, block_onehot, win_bias)


def _nsa_layer(x2, mod, norm_g, p, batch, seq):
    m, d = x2.shape
    g, hpg, dh = NSA_KV_GROUPS, NSA_HPG, NSA_HEAD_DIM
    w_a = jnp.concatenate([p["w_q"], p["w_kv_slc"], p["w_kv_win"]], axis=1).astype(BF16)
    n_gate = p["w_gate"].shape[1]
    gate_pad = 128 - n_gate
    w_b = jnp.concatenate([p["w_kv_cmp"], jnp.pad(p["w_gate"], ((0, 0), (0, gate_pad)))], axis=1).astype(BF16)
    n_cmp_cols = p["w_kv_cmp"].shape[1]
    b_b = jnp.concatenate([jnp.zeros((n_cmp_cols,), F32), jnp.pad(p["b_gate"], (0, gate_pad))])
    proj = _nm_mm(x2, mod, norm_g, w_a, jnp.zeros((w_a.shape[1],), F32), seq, BF16,
                  tm=PROJ_TILE[0], tn=PROJ_TILE[1])
    proj_b = _nm_mm(x2, mod, norm_g, w_b, b_b, seq, F32, tm=ROW_TILE, tn=w_b.shape[1], sig_from=n_cmp_cols)
    proj_b3 = proj_b.reshape(batch, seq, proj_b.shape[1])
    kv_cmp = _nsa_compress(proj_b3, p["cmp_pe"], p["cmp_w1"], p["cmp_w2"])
    gates = proj_b3[:, :, n_cmp_cols:n_cmp_cols + n_gate].reshape(batch, seq, g, 3 * hpg)
    gates = jnp.transpose(gates, (0, 2, 1, 3))
    o = _nsa_attention(proj.reshape(batch, seq, proj.shape[1]), kv_cmp, gates, seq)
    return _mm_res(o.reshape(m, d), p["w_o"].astype(BF16), x2, mod, seq)


def _rope(x, cos2, sin2):
    half = x.shape[1] // 2
    swapped = jnp.concatenate([x[:, half:], x[:, :half]], axis=-1)
    return x * cos2 + swapped * sin2


def _mla_proj_kernel(x_ref, g_ref, sc_ref, sh_ref, w_ref, qg_ref, kvg_ref, cos_ref, sin_ref,
                     wuq_ref, wuk_ref, wuv_ref, q_ref, k_ref, v_ref):
    h = _norm_mod(x_ref[...], g_ref[...], sc_ref[...], sh_ref[...]).astype(BF16)
    y = _dot(h, w_ref[...])
    r0, r1, r2 = MLA_Q_RANK, MLA_Q_RANK + MLA_KV_RANK, MLA_Q_RANK + MLA_KV_RANK + MLA_ROPE_DIM
    cos2, sin2 = cos_ref[...], sin_ref[...]
    cq = _rms(y[:, :r0], qg_ref[...]).astype(BF16)
    ckv = _rms(y[:, r0:r1], kvg_ref[...]).astype(BF16)
    kr = _rope(y[:, r1:r2], cos2, sin2).astype(BF16)
    for hd in range(q_ref.shape[0]):
        yq = _dot(cq, wuq_ref[hd])
        qr = _rope(yq[:, MLA_NOPE_DIM:], cos2, sin2)
        q_ref[hd] = jnp.concatenate([yq[:, :MLA_NOPE_DIM], qr], axis=-1).astype(BF16)
        k_ref[hd] = jnp.concatenate([_dot(ckv, wuk_ref[hd]).astype(BF16), kr], axis=-1)
        v_ref[hd] = _dot(ckv, wuv_ref[hd]).astype(BF16)


def _mla_attn_kernel(q_ref, k_ref, v_ref, o_ref, *, tq, tk):
    dqk = q_ref.shape[2]
    nh, _, dv = v_ref.shape
    assert tq == tk
    c = dqk ** -0.5 * LOG2E
    ones = jnp.ones((tk, dv), BF16)

    def tile(j, carry, diagonal):
        k0 = pl.multiple_of(j * tk, tk)
        out = []
        for h in range(nh):
            m_i, acc = carry[h]
            s = _dot_nt(q_ref[h], k_ref[h, pl.ds(k0, tk), :])
            if diagonal:
                s = jnp.where(_iota((tq, tk), 1) <= _iota((tq, tk), 0), s, NEG)
            m_new = jnp.maximum(m_i, jnp.max(s, axis=-1, keepdims=True))
            alpha = jnp.exp2((m_i - m_new) * c)
            e = jnp.exp2((s - m_new) * c).astype(BF16)
            v_aug = jnp.concatenate([v_ref[h, pl.ds(k0, tk), :], ones], axis=1)
            out.append((m_new, alpha * acc + _dot(e, v_aug)))
        return tuple(out)

    i = pl.program_id(2)
    init = tuple((jnp.full((tq, 1), NEG, F32), jnp.zeros((tq, 2 * dv), F32)) for _ in range(nh))
    carry = lax.fori_loop(0, i // 2, lambda p, cr: tile(2 * p + 1, tile(2 * p, cr, False), False), init)
    carry = lax.fori_loop(i // 2 * 2, i, lambda j, cr: tile(j, cr, False), carry)
    carry = tile(i, carry, True)
    for h in range(nh):
        acc = carry[h][1]
        o_ref[:, h * dv:(h + 1) * dv] = (acc[:, :dv] / jnp.maximum(acc[:, dv:], 1e-30)).astype(o_ref.dtype)


def _mla_layer(x2, mod, norm_g, p, batch, seq):
    m, d = x2.shape
    hh, dn, dr, dv = MLA_HEADS, MLA_NOPE_DIM, MLA_ROPE_DIM, MLA_V_DIM
    rq, rkv = MLA_Q_RANK, MLA_KV_RANK
    tm = ROW_TILE
    tps = seq // tm

    pos = jnp.arange(seq, dtype=F32)
    inv_freq = ROPE_THETA ** (-jnp.arange(0, dr, 2, dtype=F32) / dr)
    ang = pos[:, None] * inv_freq[None, :]
    cos, sin = jnp.cos(ang), jnp.sin(ang)
    cos2 = jnp.concatenate([cos, cos], axis=-1)
    sin2 = jnp.concatenate([-sin, sin], axis=-1)
    rope_spec = pl.BlockSpec((tm, dr), lambda i, *_: (i % tps, 0))

    n_down = rq + rkv + dr
    n_pad = -n_down % 128
    w_down = jnp.pad(jnp.concatenate([p["w_dq"], p["w_dkv"]], axis=1), ((0, 0), (0, n_pad))).astype(BF16)
    dqk = dn + dr
    w_uq = jnp.transpose(p["w_uq"].reshape(rq, hh, dqk), (1, 0, 2)).astype(BF16)
    w_uk = jnp.transpose(p["w_uk"].reshape(rkv, hh, dn), (1, 0, 2)).astype(BF16)
    w_uv = jnp.transpose(p["w_uv"].reshape(rkv, hh, dv), (1, 0, 2)).astype(BF16)
    whole = lambda a: pl.BlockSpec(a.shape, lambda i: (0,) * a.ndim)
    head_out = lambda width: pl.BlockSpec((None, hh, tm, width), lambda i: (i // tps, 0, i % tps, 0))
    qf, kf, vf = pl.pallas_call(
        _mla_proj_kernel,
        out_shape=(jax.ShapeDtypeStruct((batch, hh, seq, dqk), BF16),
                   jax.ShapeDtypeStruct((batch, hh, seq, dqk), BF16),
                   jax.ShapeDtypeStruct((batch, hh, seq, dv), BF16)),
        grid=(m // tm,),
        in_specs=[pl.BlockSpec((tm, d), lambda i: (i, 0)),
                  _row_spec(d, False),
                  _mod_spec(1, tps, d, False), _mod_spec(0, tps, d, False),
                  whole(w_down), _row_spec(rq, False), _row_spec(rkv, False), rope_spec, rope_spec,
                  whole(w_uq), whole(w_uk), whole(w_uv)],
        out_specs=(head_out(dqk), head_out(dqk), head_out(dv)),
        compiler_params=_params("parallel"),
        name="mla_proj",
    )(x2, norm_g.reshape(1, d), mod, mod, w_down, p["q_norm_g"].reshape(1, rq),
      p["kv_norm_g"].reshape(1, rkv), cos2, sin2, w_uq, w_uk, w_uv)

    tq = tk = MLA_SEQ_TILE
    nh = MLA_HEADS_PER_STEP
    o = pl.pallas_call(
        functools.partial(_mla_attn_kernel, tq=tq, tk=tk),
        out_shape=jax.ShapeDtypeStruct((batch, seq, hh * dv), BF16),
        grid=(batch, hh // nh, seq // tq),
        in_specs=[pl.BlockSpec((None, nh, tq, dqk), lambda b, h, i: (b, h, i, 0)),
                  pl.BlockSpec((None, nh, seq, dqk), lambda b, h, i: (b, h, 0, 0)),
                  pl.BlockSpec((None, nh, seq, dv), lambda b, h, i: (b, h, 0, 0))],
        out_specs=pl.BlockSpec((None, tq, nh * dv), lambda b, h, i: (b, i, h)),
        compiler_params=_params("parallel", "parallel", "arbitrary"),
        name="mla_attention",
    )(qf, kf, vf)
    return _mm_res(o.reshape(m, hh * dv), p["w_o"].astype(BF16), x2, mod, seq)


def kernel(x, c, ada_w, ada_b, norm1_g, norm2_g, mlp_w1, mlp_w2, final_g, conv_w_in, conv_b_in, conv_w_dw, conv_b_dw, conv_ln_g, conv_ln_b, conv_w_out, conv_b_out, nsa_w_q, nsa_w_kv_cmp, nsa_w_kv_slc, nsa_w_kv_win, nsa_cmp_pe, nsa_cmp_w1, nsa_cmp_w2, nsa_w_gate, nsa_b_gate, nsa_w_o, pool_w, pool_b, pool_scale, mla_w_dq, mla_q_norm_g, mla_w_uq, mla_w_dkv, mla_kv_norm_g, mla_w_uk, mla_w_uv, mla_w_o):
    batch, seq, d = x.shape
    depth = ada_w.shape[0]
    n_mixers = 4
    mods = _ada_mod(c, ada_w, ada_b)
    x2 = x.reshape(batch * seq, d)
    w1 = w2 = None
    for i in range(depth):
        kind, u = i % n_mixers, i // n_mixers
        mod = mods[i].reshape(batch * 6, 1, d)
        if kind == 0:
            p = dict(w_in=conv_w_in[u], b_in=conv_b_in[u], w_dw=conv_w_dw[u], b_dw=conv_b_dw[u],
                     ln_g=conv_ln_g[u], ln_b=conv_ln_b[u], w_out=conv_w_out[u], b_out=conv_b_out[u])
            x2, w1, w2 = _conv_layer(x2, mod, norm1_g[i], p, seq, (mlp_w1, mlp_w2, i))
        elif kind == 1:
            p = dict(w_q=nsa_w_q[u], w_kv_cmp=nsa_w_kv_cmp[u], w_kv_slc=nsa_w_kv_slc[u],
                     w_kv_win=nsa_w_kv_win[u], cmp_pe=nsa_cmp_pe[u], cmp_w1=nsa_cmp_w1[u],
                     cmp_w2=nsa_cmp_w2[u], w_gate=nsa_w_gate[u], b_gate=nsa_b_gate[u], w_o=nsa_w_o[u])
            x2 = _nsa_layer(x2, mod, norm1_g[i], p, batch, seq)
        elif kind == 2:
            p = dict(w=pool_w[u], b=pool_b[u], scale=pool_scale[u])
            x2 = _pool_layer(x2, mod, norm1_g[i], p, seq)
        else:
            p = dict(w_dq=mla_w_dq[u], q_norm_g=mla_q_norm_g[u], w_uq=mla_w_uq[u], w_dkv=mla_w_dkv[u],
                     kv_norm_g=mla_kv_norm_g[u], w_uk=mla_w_uk[u], w_uv=mla_w_uv[u], w_o=mla_w_o[u])
            x2 = _mla_layer(x2, mod, norm1_g[i], p, batch, seq)
        if i == depth - 1:
            x2 = _mlp(x2, mod, norm2_g[i], w1, w2, seq, final_g=final_g)
        else:
            x2, w1, w2 = _mlp(x2, mod, norm2_g[i], w1, w2, seq, next_f32=(mlp_w1, mlp_w2, i + 1))
    return x2.reshape(batch, seq, d)
```
